```python
import math, functools
import jax, jax.numpy as jnp
from jax import lax
import numpy as np

D_MODEL = 1024
BATCH = 2
SEQ = 8192
DEPTH = 2
DEC_BATCH = 128
DEC_SEQ = 1
PAST_LEN = 2048
PAGE_SIZE = 128

D_MIX = 2 * D_MODEL
C_A = D_MIX // 4
CONV_A_W = 31
H_B = 8
DH_B = 64
D_B = H_B * DH_B
SB_BLOCK = 128
D_INNER = D_MIX // 2
H_C = 16
P_C = D_INNER // H_C
G_C = 2
N_C = 128
CONV_C_W = 4
CONV_C_DIM = D_INNER + 2 * G_C * N_C
SSD_CHUNK = 128
D_FF = int(math.ceil(8 * D_MODEL / 3 / 256)) * 256
D_PLE = 256
EPS = 1e-6
SPLIT_IDX = (2 * C_A, 2 * C_A + D_B, 2 * C_A + 2 * D_B, 2 * C_A + 3 * D_B,
             2 * C_A + 3 * D_B + D_INNER, 2 * C_A + 3 * D_B + D_INNER + CONV_C_DIM)
D_IN_PROJ = 2 * C_A + 3 * D_B + D_INNER + CONV_C_DIM + H_C

kernel_name = 'hybrid_conformer_stickbreak_ssd_step'


def rms_norm(x, g):
    xf = x.astype(jnp.float32)
    y = xf * lax.rsqrt(jnp.mean(xf * xf, axis=-1, keepdims=True) + EPS)
    return (y * g.astype(jnp.float32)).astype(x.dtype)


def layer_norm(x, g, b):
    xf = x.astype(jnp.float32)
    mu = jnp.mean(xf, axis=-1, keepdims=True)
    xc = xf - mu
    y = xc * lax.rsqrt(jnp.mean(xc * xc, axis=-1, keepdims=True) + EPS)
    return (y * g.astype(jnp.float32) + b.astype(jnp.float32)).astype(x.dtype)


def causal_dwconv(u, prev, w, b):
    W = w.shape[0]
    full = jnp.concatenate([prev.astype(u.dtype), u], axis=1)
    y = lax.conv_general_dilated(full, w[:, None, :].astype(u.dtype), (1,), 'VALID',
                                 dimension_numbers=('NWC', 'WIO', 'NWC'),
                                 feature_group_count=u.shape[-1])
    return y + b, full[:, full.shape[1] - (W - 1):]


def stick_breaking(q, k, v, bias, q_pos, k_pos):
    z = jnp.einsum('bqhd,bkhd->bhqk', q, k).astype(jnp.float32) * (DH_B ** -0.5)
    z = z + bias.astype(jnp.float32)[None, :, None, None]
    mask = k_pos[None, :] < q_pos[:, None]
    log_fail = jnp.where(mask, jax.nn.log_sigmoid(-z), 0.0)
    later = lax.cumsum(log_fail, axis=3, reverse=True) - log_fail
    w = jnp.where(mask, jnp.exp(jax.nn.log_sigmoid(z) + later), 0.0)
    return jnp.einsum('bhqk,bkhd->bqhd', w.astype(v.dtype), v)


def sb_prompt(q, k, v, bias):
    bt, L, H, D = q.shape
    nb = L // SB_BLOCK
    qb = jnp.swapaxes(q.reshape(bt, nb, SB_BLOCK, H, D), 0, 1)
    k_pos = jnp.arange(L, dtype=jnp.int32)
    def one_block(args):
        qi, bi = args
        q_pos = bi * SB_BLOCK + jnp.arange(SB_BLOCK, dtype=jnp.int32)
        return stick_breaking(qi, k, v, bias, q_pos, k_pos)
    out = lax.map(one_block, (qb, jnp.arange(nb, dtype=jnp.int32)))
    return jnp.swapaxes(out, 0, 1).reshape(bt, L, H, D)


def sb_sample(q, k, v, bias, past_k, past_v):
    P = past_k.shape[1]
    Lq = q.shape[1]
    k_all = jnp.concatenate([past_k.astype(k.dtype), k], axis=1)
    v_all = jnp.concatenate([past_v.astype(v.dtype), v], axis=1)
    k_pos = jnp.arange(P + Lq, dtype=jnp.int32)
    q_pos = P + jnp.arange(Lq, dtype=jnp.int32)
    return stick_breaking(q, k_all, v_all, bias, q_pos, k_pos)


def ssd(x, dt, A, B, C, h0):
    bt, L, H, P = x.shape
    Q = SSD_CHUNK if L % SSD_CHUNK == 0 else L
    nc = L // Q
    Hg = H // G_C
    f32 = jnp.float32
    xdt = (x.astype(f32) * dt[..., None]).reshape(bt, nc, Q, G_C, Hg, P)
    a_cum = jnp.cumsum((dt * A).reshape(bt, nc, Q, G_C, Hg), axis=2)
    Bc = B.astype(f32).reshape(bt, nc, Q, G_C, N_C)
    Cc = C.astype(f32).reshape(bt, nc, Q, G_C, N_C)
    causal = jnp.tril(jnp.ones((Q, Q), dtype=bool))
    seg = a_cum[:, :, :, None] - a_cum[:, :, None, :]
    decay_in = jnp.exp(jnp.where(causal[None, None, :, :, None, None], seg, -jnp.inf))
    cb = jnp.einsum('bclgn,bcsgn->bclsg', Cc, Bc)
    y_diag = jnp.einsum('bclsg,bclsgh,bcsghp->bclghp', cb, decay_in, xdt)
    decay_out = jnp.exp(a_cum[:, :, -1:] - a_cum)
    states = jnp.einsum('bcsgn,bcsgh,bcsghp->bcghpn', Bc, decay_out, xdt)
    chunk_decay = jnp.exp(a_cum[:, :, -1])
    def step(hc, inp):
        dec, st = inp
        return dec[..., None, None] * hc + st, hc
    h_init = h0.astype(f32).reshape(bt, G_C, Hg, P, N_C)
    h_last, h_prev = lax.scan(step, h_init, (jnp.moveaxis(chunk_decay, 1, 0), jnp.moveaxis(states, 1, 0)))
    h_prev = jnp.moveaxis(h_prev, 0, 1)
    y_off = jnp.einsum('bclgn,bcghpn,bclgh->bclghp', Cc, h_prev, jnp.exp(a_cum))
    y = (y_diag + y_off).reshape(bt, L, H, P)
    return y.astype(x.dtype), h_last.reshape(bt, H, P, N_C).astype(h0.dtype)


def trunk_layer(h, pe, lp, conv_a_prev, conv_c_prev, ssm_prev, attend):
    bt, L, _ = h.shape
    u = rms_norm(h, lp['g_mix'])
    proj = u @ lp['w_in']
    a_in, q, k, v, z, xbc, dt = jnp.split(proj, SPLIT_IDX, axis=-1)
    a_val, a_gate = jnp.split(a_in, 2, axis=-1)
    a_glu = a_val * jax.nn.sigmoid(a_gate)
    a_conv, conv_a_new = causal_dwconv(a_glu, conv_a_prev, lp['conv_a_w'], lp['conv_a_b'])
    a_out = jax.nn.silu(layer_norm(a_conv, lp['ln_a_g'], lp['ln_a_b']))
    q = rms_norm(q.reshape(bt, L, H_B, DH_B), lp['g_q'])
    k = rms_norm(k.reshape(bt, L, H_B, DH_B), lp['g_k'])
    v = v.reshape(bt, L, H_B, DH_B)
    b_out = attend(q, k, v, lp['sb_bias']).reshape(bt, L, D_B)
    xbc_c, conv_c_new = causal_dwconv(xbc, conv_c_prev, lp['conv_ssm_w'], lp['conv_ssm_b'])
    xbc_c = jax.nn.silu(xbc_c)
    xs, Bs, Cs = jnp.split(xbc_c, (D_INNER, D_INNER + G_C * N_C), axis=-1)
    dt = jax.nn.softplus(dt.astype(jnp.float32) + lp['dt_bias'].astype(jnp.float32))
    A = -jnp.exp(lp['a_log'].astype(jnp.float32))
    xh = xs.reshape(bt, L, H_C, P_C)
    y, ssm_new = ssd(xh, dt, A, Bs.reshape(bt, L, G_C, N_C), Cs.reshape(bt, L, G_C, N_C), ssm_prev)
    y = (y + lp['d_skip'][:, None] * xh).reshape(bt, L, D_INNER) * jax.nn.silu(z)
    c_out = rms_norm(y.reshape(bt, L, G_C, D_INNER // G_C),
                     lp['g_ssm'].reshape(G_C, D_INNER // G_C)).reshape(bt, L, D_INNER)
    h = h + jnp.concatenate([a_out, b_out, c_out], axis=-1) @ lp['w_out']
    f = rms_norm(h, lp['g_ffn'])
    h = h + (jax.nn.silu(f @ lp['w_ffn_gate']) * (f @ lp['w_ffn_up'])) @ lp['w_ffn_down']
    gate = jax.nn.sigmoid(rms_norm(h, lp['g_ple']) @ lp['w_ple_gate'])
    h = h + (pe @ lp['w_ple']) * gate
    return h, k, v, conv_a_new, conv_c_new, ssm_new


def setup_inputs(seed: int = 0) -> dict:
    key = jax.random.key(seed)
    ks = iter(jax.random.split(key, 48))
    f32 = jnp.float32
    nrm = lambda shape, s: jax.random.normal(next(ks), shape, f32) * s
    gain = lambda shape: 1.0 + nrm(shape, 0.01)
    n_pages = PAST_LEN // PAGE_SIZE
    n_pool = (5 * DEC_BATCH * n_pages + 3) // 4
    page_table = jax.random.permutation(next(ks), n_pool)[:DEC_BATCH * n_pages]
    page_table = page_table.reshape(DEC_BATCH, n_pages).astype(jnp.int32)
    dt0 = jnp.exp(jax.random.uniform(next(ks), (DEPTH, H_C), f32) * (math.log(0.1) - math.log(0.001)) + math.log(0.001))
    dt_bias = dt0 + jnp.log(-jnp.expm1(-dt0))
    a_log = jnp.log(jax.random.uniform(next(ks), (DEPTH, H_C), f32, 1.0, 16.0))
    sb_bias = jax.random.uniform(next(ks), (DEPTH, H_B), f32, -8.0, -5.0)
    return {
        'x_prompt': nrm((BATCH, SEQ, D_MODEL), 1.0),
        'x_sample': nrm((DEC_BATCH, DEC_SEQ, D_MODEL), 1.0),
        'cache_k': nrm((DEPTH, n_pool, PAGE_SIZE, H_B, DH_B), 1.0),
        'cache_v': nrm((DEPTH, n_pool, PAGE_SIZE, H_B, DH_B), 1.0),
        'state_conv_a': nrm((DEPTH, DEC_BATCH, CONV_A_W - 1, C_A), 1.0),
        'state_conv_ssm': nrm((DEPTH, DEC_BATCH, CONV_C_W - 1, CONV_C_DIM), 1.0),
        'state_ssm': nrm((DEPTH, DEC_BATCH, H_C, P_C, N_C), 0.5),
        'page_table': page_table,
        'p_prompt': nrm((DEPTH, BATCH, SEQ, D_PLE), 1.0),
        'p_sample': nrm((DEPTH, DEC_BATCH, DEC_SEQ, D_PLE), 1.0),
        'g_mix': gain((DEPTH, D_MODEL)),
        'w_in': nrm((DEPTH, D_MODEL, D_IN_PROJ), D_MODEL ** -0.5),
        'conv_a_w': nrm((DEPTH, CONV_A_W, C_A), CONV_A_W ** -0.5),
        'conv_a_b': nrm((DEPTH, C_A), 0.01),
        'ln_a_g': gain((DEPTH, C_A)),
        'ln_a_b': nrm((DEPTH, C_A), 0.01),
        'g_q': gain((DEPTH, DH_B)),
        'g_k': gain((DEPTH, DH_B)),
        'sb_bias': sb_bias,
        'conv_ssm_w': nrm((DEPTH, CONV_C_W, CONV_C_DIM), CONV_C_W ** -0.5),
        'conv_ssm_b': nrm((DEPTH, CONV_C_DIM), 0.01),
        'dt_bias': dt_bias,
        'a_log': a_log,
        'd_skip': gain((DEPTH, H_C)),
        'g_ssm': gain((DEPTH, D_INNER)),
        'w_out': nrm((DEPTH, D_MIX, D_MODEL), D_MIX ** -0.5),
        'g_ffn': gain((DEPTH, D_MODEL)),
        'w_ffn_gate': nrm((DEPTH, D_MODEL, D_FF), D_MODEL ** -0.5),
        'w_ffn_up': nrm((DEPTH, D_MODEL, D_FF), D_MODEL ** -0.5),
        'w_ffn_down': nrm((DEPTH, D_FF, D_MODEL), D_FF ** -0.5),
        'g_ple': gain((DEPTH, D_MODEL)),
        'w_ple_gate': nrm((DEPTH, D_MODEL, D_MODEL), D_MODEL ** -0.5),
        'w_ple': nrm((DEPTH, D_PLE, D_MODEL), D_PLE ** -0.5),
    }


def reference(x_prompt, x_sample, cache_k, cache_v, state_conv_a, state_conv_ssm, state_ssm,
              page_table, p_prompt, p_sample, g_mix, w_in, conv_a_w, conv_a_b, ln_a_g, ln_a_b,
              g_q, g_k, sb_bias, conv_ssm_w, conv_ssm_b, dt_bias, a_log, d_skip, g_ssm, w_out, g_ffn,
              w_ffn_gate, w_ffn_up, w_ffn_down, g_ple, w_ple_gate, w_ple):
    bp = x_prompt.shape[0]
    n_seq = page_table.shape[0]
    h_p = x_prompt
    h_s = x_sample
    kp_l, vp_l, cap_l, ccp_l, ssp_l = [], [], [], [], []
    ks_l, vs_l, cas_l, ccs_l, sss_l = [], [], [], [], []
    for i in range(DEPTH):
        lp = dict(g_mix=g_mix[i], w_in=w_in[i], conv_a_w=conv_a_w[i], conv_a_b=conv_a_b[i],
                  ln_a_g=ln_a_g[i], ln_a_b=ln_a_b[i], g_q=g_q[i], g_k=g_k[i], sb_bias=sb_bias[i],
                  conv_ssm_w=conv_ssm_w[i], conv_ssm_b=conv_ssm_b[i], dt_bias=dt_bias[i],
                  a_log=a_log[i], d_skip=d_skip[i], g_ssm=g_ssm[i], w_out=w_out[i], g_ffn=g_ffn[i],
                  w_ffn_gate=w_ffn_gate[i], w_ffn_up=w_ffn_up[i], w_ffn_down=w_ffn_down[i],
                  g_ple=g_ple[i], w_ple_gate=w_ple_gate[i], w_ple=w_ple[i])
        h_p, kp, vp, cap, ccp, ssp = trunk_layer(
            h_p, p_prompt[i], lp,
            jnp.zeros((bp, CONV_A_W - 1, C_A), x_prompt.dtype),
            jnp.zeros((bp, CONV_C_W - 1, CONV_C_DIM), x_prompt.dtype),
            jnp.zeros((bp, H_C, P_C, N_C), x_prompt.dtype),
            sb_prompt)
        past_k = cache_k[i][page_table].reshape(n_seq, -1, H_B, DH_B)
        past_v = cache_v[i][page_table].reshape(n_seq, -1, H_B, DH_B)
        h_s, ksn, vsn, cas, ccs, sss = trunk_layer(
            h_s, p_sample[i], lp, state_conv_a[i], state_conv_ssm[i], state_ssm[i],
            functools.partial(sb_sample, past_k=past_k, past_v=past_v))
        kp_l.append(kp); vp_l.append(vp); cap_l.append(cap); ccp_l.append(ccp); ssp_l.append(ssp)
        ks_l.append(ksn); vs_l.append(vsn); cas_l.append(cas); ccs_l.append(ccs); sss_l.append(sss)
    return (h_p, h_s,
            jnp.stack(kp_l), jnp.stack(vp_l), jnp.stack(cap_l), jnp.stack(ccp_l), jnp.stack(ssp_l),
            jnp.stack(ks_l), jnp.stack(vs_l), jnp.stack(cas_l), jnp.stack(ccs_l), jnp.stack(sss_l))
```

```python
import functools
import math

import jax
import jax.numpy as jnp
from jax import lax
from jax.experimental import pallas as pl
from jax.experimental.pallas import tpu as pltpu

F32 = jnp.float32
BF16 = jnp.bfloat16
HIGHEST = lax.Precision.HIGHEST

D_MODEL = 1024
C_A = 512
CONV_A_W = 31
H_B = 8
DH_B = 64
D_B = H_B * DH_B
D_INNER = 1024
H_C = 16
P_C = 64
G_C = 2
N_C = 128
CONV_C_W = 4
CONV_C_DIM = D_INNER + 2 * G_C * N_C
D_FF = 2816
D_PLE = 256
EPS = 1e-6
PAGE = 128

LANES = 128
SUBLANES = 8
VMEM_LIMIT = 56 * 1024 * 1024

OFF_A = 0
OFF_Q = 2 * C_A
OFF_K = OFF_Q + D_B
OFF_V = OFF_K + D_B
OFF_Z = OFF_V + D_B
OFF_X = OFF_Z + D_INNER
OFF_DT = OFF_X + CONV_C_DIM
W_IN_COLS = OFF_DT + LANES

FF_CHUNK = 256
SSD_CHUNK = 128
CONF_TILE = 512
CONF_ROWS = 32
CONF_PAD = 32
ATT_TILE = 256
HEADS_PER_STEP = LANES // DH_B


def _const_spec(shape):
    zeros = (0,) * len(shape)
    return pl.BlockSpec(shape, lambda *_: zeros, pipeline_mode=pl.Buffered(1))


def _params(*sem):
    return pltpu.CompilerParams(dimension_semantics=sem, vmem_limit_bytes=VMEM_LIMIT)


def _sigmoid(x):
    return 1.0 / (1.0 + jnp.exp(-x))


def _silu(x):
    return x * _sigmoid(x)


def _softplus(x):
    return jnp.maximum(x, 0.0) + jnp.log(1.0 + jnp.exp(-jnp.abs(x)))


def _rms(x, g):
    return x * lax.rsqrt(jnp.mean(x * x, axis=-1, keepdims=True) + EPS) * g


def _dot(a, b):
    return jnp.dot(a, b, preferred_element_type=F32)


def _dot_nt(a, b):
    return lax.dot_general(a, b, (((1,), (1,)), ((), ())), preferred_element_type=F32)


def _dot_exact(a, b):
    return jnp.dot(a, b, precision=HIGHEST, preferred_element_type=F32)


def _split_dot(x, m):
    hi = x.astype(BF16)
    lo = (x - hi.astype(F32)).astype(BF16)
    return _dot(hi, m) + _dot(lo, m)


def _in_proj_kernel(h_ref, g_ref, w_ref, gq_ref, gk_ref, seg_ref,
                    a_ref, q_ref, k_ref, v_ref, kb_ref, vb_ref, z_ref, x_ref, dt_ref):
    u = _rms(h_ref[...], g_ref[...]).astype(BF16)

    def proj(lo, hi):
        return _dot(u, w_ref[:, lo:hi])

    a_ref[...] = proj(OFF_A, OFF_Q)
    z_ref[...] = proj(OFF_Z, OFF_X)
    x_ref[...] = proj(OFF_X, OFF_DT)
    dt_ref[...] = proj(OFF_DT, W_IN_COLS)

    seg = seg_ref[...]

    def head_norm(t, g):
        ms = _split_dot(t * t, seg) * (1.0 / DH_B)
        return t * lax.rsqrt(ms + EPS) * g

    qn = head_norm(proj(OFF_Q, OFF_K), gq_ref[...])
    kn = head_norm(proj(OFF_K, OFF_V), gk_ref[...])
    v = proj(OFF_V, OFF_Z)
    q_ref[...] = (qn * (DH_B ** -0.5)).astype(BF16)
    k_ref[...] = kn
    v_ref[...] = v
    kb_ref[...] = kn.astype(BF16)
    vb_ref[...] = v.astype(BF16)


def _in_proj(h, g_mix, w_in, gq, gk, seg, tm):
    t = h.shape[0]
    row = lambda n: pl.BlockSpec((tm, n), lambda i: (i, 0))
    widths = (2 * C_A, D_B, D_B, D_B, D_B, D_B, D_INNER, CONV_C_DIM, LANES)
    dtypes = (F32, BF16, F32, F32, BF16, BF16, F32, F32, F32)
    return pl.pallas_call(
        _in_proj_kernel,
        grid=(t // tm,),
        in_specs=[row(D_MODEL), _const_spec((1, D_MODEL)), _const_spec((D_MODEL, W_IN_COLS)),
                  _const_spec((1, D_B)), _const_spec((1, D_B)), _const_spec((D_B, D_B))],
        out_specs=[row(n) for n in widths],
        out_shape=[jax.ShapeDtypeStruct((t, n), d) for n, d in zip(widths, dtypes)],
        compiler_params=_params("parallel"),
        name="in_proj",
    )(h, g_mix, w_in, gq, gk, seg)


def _layer_norm_silu(x, g, b):
    mu = jnp.mean(x, axis=-1, keepdims=True)
    xc = x - mu
    y = xc * lax.rsqrt(jnp.mean(xc * xc, axis=-1, keepdims=True) + EPS)
    return _silu(y * g + b)


def _conf_prompt_kernel(a_ref, w_ref, b_ref, g_ref, bb_ref, o_ref, tail_ref, buf):
    tile = a_ref.shape[0]

    @pl.when(pl.program_id(1) == 0)
    def _():
        buf[0:CONF_PAD, :] = jnp.zeros((CONF_PAD, C_A), F32)

    a = a_ref[...]
    buf[CONF_PAD:CONF_PAD + tile, :] = a[:, :C_A] * _sigmoid(a[:, C_A:])
    first = CONF_PAD - (CONV_A_W - 1)
    for r0 in range(0, tile, CONF_ROWS):
        acc = jnp.zeros((CONF_ROWS, C_A), F32)
        for k in range(CONV_A_W):
            acc = acc + w_ref[k:k + 1, :] * buf[first + r0 + k:first + r0 + k + CONF_ROWS, :]
        y = _layer_norm_silu(acc + b_ref[...], g_ref[...], bb_ref[...])
        o_ref[r0:r0 + CONF_ROWS, :] = y.astype(BF16)
    tail = buf[tile:tile + CONF_PAD, :]
    buf[0:CONF_PAD, :] = tail
    tail_ref[0] = tail


def _conf_prompt(a_in, w, b, g, bb, batch):
    t = a_in.shape[0]
    nt = t // batch // CONF_TILE
    return pl.pallas_call(
        _conf_prompt_kernel,
        grid=(batch, nt),
        in_specs=[pl.BlockSpec((CONF_TILE, 2 * C_A), lambda bi, ti: (bi * nt + ti, 0)),
                  _const_spec((CONF_PAD, C_A)), _const_spec((1, C_A)), _const_spec((1, C_A)),
                  _const_spec((1, C_A))],
        out_specs=[pl.BlockSpec((CONF_TILE, C_A), lambda bi, ti: (bi * nt + ti, 0)),
                   pl.BlockSpec((1, CONF_PAD, C_A), lambda bi, ti: (bi, 0, 0))],
        out_shape=[jax.ShapeDtypeStruct((t, C_A), BF16),
                   jax.ShapeDtypeStruct((batch, CONF_PAD, C_A), F32)],
        scratch_shapes=[pltpu.VMEM((CONF_PAD + CONF_TILE, C_A), F32)],
        compiler_params=_params("parallel", "arbitrary"),
        name="conformer_prompt",
    )(a_in, w, b, g, bb)


def _attn_prompt_kernel(bias_ref, q_ref, k_ref, v_ref, tri_ref, o_ref):
    pair = pl.program_id(1)
    qi = pl.program_id(2)
    tile = q_ref.shape[0]
    q2 = q_ref[...].astype(F32)
    tri = tri_ref[...]
    lane = lax.broadcasted_iota(jnp.int32, (tile, LANES), 1)
    row = lax.broadcasted_iota(jnp.int32, (tile, tile), 0)
    col = lax.broadcasted_iota(jnp.int32, (tile, tile), 1)
    causal = col < row

    def key_tile(qh, bias, start, carry, acc, masked):
        kt = k_ref[pl.ds(start, tile), :]
        vt = v_ref[pl.ds(start, tile), :]
        s = _dot_nt(qh, kt) + bias
        sp = _softplus(s)
        if masked:
            sp = jnp.where(causal, sp, 0.0)
        cs = _split_dot(sp, tri)
        w = jnp.exp(s - cs - carry)
        if masked:
            w = jnp.where(causal, w, 0.0)
        acc = acc + _dot(w.astype(BF16), vt)
        return carry + cs[:, 0:1], acc

    outs = []
    for hh in range(HEADS_PER_STEP):
        qh = jnp.where((lane // DH_B) == hh, q2, 0.0).astype(BF16)
        bias = bias_ref[pair * HEADS_PER_STEP + hh]
        start0 = pl.multiple_of(qi * tile, tile)
        carry, acc = key_tile(qh, bias, start0, jnp.zeros((tile, 1), F32),
                              jnp.zeros((tile, LANES), F32), True)

        def body(j, c, qh=qh, bias=bias):
            start = pl.multiple_of((qi - 1 - j) * tile, tile)
            return key_tile(qh, bias, start, c[0], c[1], False)

        carry, acc = lax.fori_loop(0, qi, body, (carry, acc))
        outs.append(acc)
    out = outs[0]
    for hh in range(1, HEADS_PER_STEP):
        out = jnp.where((lane // DH_B) == hh, outs[hh], out)
    o_ref[...] = out.astype(BF16)


def _attn_prompt(bias, q, k, v, tri, batch):
    t = q.shape[0]
    seq = t // batch
    nq = seq // ATT_TILE
    grid_spec = pltpu.PrefetchScalarGridSpec(
        num_scalar_prefetch=1,
        grid=(batch, H_B // HEADS_PER_STEP, nq),
        in_specs=[pl.BlockSpec((ATT_TILE, LANES), lambda b, p, i, *_: (b * nq + i, p)),
                  pl.BlockSpec((seq, LANES), lambda b, p, i, *_: (b, p)),
                  pl.BlockSpec((seq, LANES), lambda b, p, i, *_: (b, p)),
                  pl.BlockSpec((ATT_TILE, ATT_TILE), lambda b, p, i, *_: (0, 0))],
        out_specs=pl.BlockSpec((ATT_TILE, LANES), lambda b, p, i, *_: (b * nq + i, p)),
    )
    return pl.pallas_call(
        _attn_prompt_kernel,
        grid_spec=grid_spec,
        out_shape=jax.ShapeDtypeStruct((t, D_B), BF16),
        compiler_params=_params("parallel", "parallel", "arbitrary"),
        name="attn_prompt",
    )(bias, q, k, v, tri)


def _gated_group_norm(y, xs, z, dskip, g):
    y = (y + dskip * xs) * _silu(z)
    gw = D_INNER // G_C
    return jnp.concatenate(
        [_rms(y[:, i * gw:(i + 1) * gw], g[:, i * gw:(i + 1) * gw]) for i in range(G_C)], axis=1)


def _ssd_prompt_kernel(x_ref, z_ref, dt_ref, cw_ref, cb_ref, dtb_ref, alog_ref, dskip_ref, g_ref,
                       tril_ref, e_ref, c_ref, st_ref, cbuf, ybuf):
    q = x_ref.shape[0]

    @pl.when(pl.program_id(1) == 0)
    def _():
        cbuf[0:SUBLANES, :] = jnp.zeros((SUBLANES, CONV_C_DIM), F32)
        st_ref[...] = jnp.zeros_like(st_ref)

    cbuf[SUBLANES:SUBLANES + q, :] = x_ref[...]
    first = SUBLANES - (CONV_C_W - 1)
    conv = cb_ref[...] + sum(cw_ref[k:k + 1, :] * cbuf[first + k:first + k + q, :]
                             for k in range(CONV_C_W))
    cbuf[0:SUBLANES, :] = cbuf[q:q + SUBLANES, :]
    xbc = _silu(conv)
    xs = xbc[:, :D_INNER]
    bm = xbc[:, D_INNER:D_INNER + G_C * N_C].astype(BF16)
    cm = xbc[:, D_INNER + G_C * N_C:].astype(BF16)

    dt = _softplus(dt_ref[...] + dtb_ref[...])
    a_cum = _dot_exact(tril_ref[...], dt * (-jnp.exp(alog_ref[...])))
    e = e_ref[...]
    dt_x = _dot_exact(dt, e)
    a_x = _dot_exact(a_cum, e)
    a_cum_t = a_cum.T
    a_xt = a_x.T
    xdt = xs * dt_x
    exp_a_x = jnp.exp(a_x)

    row = lax.broadcasted_iota(jnp.int32, (q, q), 0)
    col = lax.broadcasted_iota(jnp.int32, (q, q), 1)
    causal = col <= row
    lane = lax.broadcasted_iota(jnp.int32, (q, LANES), 1)
    heads_per_group = H_C // G_C

    for g in range(G_C):
        bg = bm[:, g * N_C:(g + 1) * N_C]
        cg = cm[:, g * N_C:(g + 1) * N_C]
        cb = _dot_nt(cg, bg)
        for pp in range(heads_per_group // HEADS_PER_STEP):
            pair = g * (heads_per_group // HEADS_PER_STEP) + pp
            lo = pair * LANES
            xdt_p = xdt[:, lo:lo + LANES]
            xdt_pb = xdt_p.astype(BF16)
            y_p = None
            for hh in range(HEADS_PER_STEP):
                h = pair * HEADS_PER_STEP + hh
                seg = a_cum[:, h:h + 1] - a_cum_t[h:h + 1, :]
                m = (cb * jnp.exp(jnp.where(causal, seg, -jnp.inf))).astype(BF16)
                r = _dot(m, xdt_pb)
                y_p = r if y_p is None else jnp.where((lane // P_C) == hh, r, y_p)
            h_prev = st_ref[0, lo:lo + LANES, :]
            y_p = y_p + _dot_nt(cg, h_prev.astype(BF16)) * exp_a_x[:, lo:lo + LANES]
            ybuf[:, lo:lo + LANES] = y_p
            a_t = a_xt[lo:lo + LANES, :]
            tot = a_t[:, q - 1:q]
            s_new = _dot((xdt_p.T * jnp.exp(tot - a_t)).astype(BF16), bg)
            st_ref[0, lo:lo + LANES, :] = jnp.exp(tot) * h_prev + s_new

    c_ref[...] = _gated_group_norm(ybuf[...], xs, z_ref[...], dskip_ref[...], g_ref[...]).astype(BF16)


def _ssd_prompt(xbc, z, dt, cw, cb, dtb, alog, dskip, g, tril, e, batch):
    t = xbc.shape[0]
    nc = t // batch // SSD_CHUNK
    row = lambda n: pl.BlockSpec((SSD_CHUNK, n), lambda bi, ci: (bi * nc + ci, 0))
    return pl.pallas_call(
        _ssd_prompt_kernel,
        grid=(batch, nc),
        in_specs=[row(CONV_C_DIM), row(D_INNER), row(LANES),
                  _const_spec((SUBLANES, CONV_C_DIM)), _const_spec((1, CONV_C_DIM)),
                  _const_spec((1, LANES)), _const_spec((1, LANES)), _const_spec((1, D_INNER)),
                  _const_spec((1, D_INNER)), _const_spec((SSD_CHUNK, SSD_CHUNK)),
                  _const_spec((LANES, D_INNER))],
        out_specs=[row(D_INNER), pl.BlockSpec((1, D_INNER, N_C), lambda bi, ci: (bi, 0, 0))],
        out_shape=[jax.ShapeDtypeStruct((t, D_INNER), BF16),
                   jax.ShapeDtypeStruct((batch, D_INNER, N_C), F32)],
        scratch_shapes=[pltpu.VMEM((SUBLANES + SSD_CHUNK, CONV_C_DIM), F32),
                        pltpu.VMEM((SSD_CHUNK, D_INNER), F32)],
        compiler_params=_params("parallel", "arbitrary"),
        name="ssd_prompt",
    )(xbc, z, dt, cw, cb, dtb, alog, dskip, g, tril, e)


def _post_kernel(h_ref, a_ref, b_ref, c_ref, pe_ref, wo_ref, gf_ref, wg_ref, wu_ref, wd_ref,
                 gp_ref, wpg_ref, wp_ref, o_ref):
    h = (h_ref[...] + _dot(a_ref[...], wo_ref[0:C_A, :]) + _dot(b_ref[...], wo_ref[C_A:C_A + D_B, :])
         + _dot(c_ref[...], wo_ref[C_A + D_B:, :]))
    f = _rms(h, gf_ref[...]).astype(BF16)
    acc = jnp.zeros_like(h)
    for c0 in range(0, D_FF, FF_CHUNK):
        gate = _dot(f, wg_ref[:, c0:c0 + FF_CHUNK])
        up = _dot(f, wu_ref[:, c0:c0 + FF_CHUNK])
        acc = acc + _dot((_silu(gate) * up).astype(BF16), wd_ref[c0:c0 + FF_CHUNK, :])
    h = h + acc
    gate = _sigmoid(_dot(_rms(h, gp_ref[...]).astype(BF16), wpg_ref[...]))
    o_ref[...] = h + _dot(pe_ref[...].astype(BF16), wp_ref[...]) * gate


def _post(h, a, b, c, pe, wo, gf, wg, wu, wd, gp, wpg, wp, tm):
    t = h.shape[0]
    row = lambda n: pl.BlockSpec((tm, n), lambda i: (i, 0))
    return pl.pallas_call(
        _post_kernel,
        grid=(t // tm,),
        in_specs=[row(D_MODEL), row(C_A), row(D_B), row(D_INNER), row(D_PLE),
                  _const_spec(wo.shape), _const_spec((1, D_MODEL)), _const_spec(wg.shape),
                  _const_spec(wu.shape), _const_spec(wd.shape), _const_spec((1, D_MODEL)),
                  _const_spec(wpg.shape), _const_spec(wp.shape)],
        out_specs=row(D_MODEL),
        out_shape=jax.ShapeDtypeStruct((t, D_MODEL), F32),
        compiler_params=_params("parallel"),
        name="post",
    )(h, a, b, c, pe, wo, gf, wg, wu, wd, gp, wpg, wp)


def _conf_sample_kernel(a_ref, st_ref, w_ref, b_ref, g_ref, bb_ref, o_ref, nst_ref):
    a = a_ref[...]
    glu = a[:, :C_A] * _sigmoid(a[:, C_A:])
    hist = CONV_A_W - 1
    acc = b_ref[...] + w_ref[hist:hist + 1, :] * glu
    for k in range(hist):
        acc = acc + w_ref[k:k + 1, :] * st_ref[:, k * C_A:(k + 1) * C_A]
    o_ref[...] = _layer_norm_silu(acc, g_ref[...], bb_ref[...]).astype(BF16)
    nst_ref[:, 0:(hist - 1) * C_A] = st_ref[:, C_A:hist * C_A]
    nst_ref[:, (hist - 1) * C_A:] = glu


def _conf_sample(a_in, state, w, b, g, bb, nb):
    n = a_in.shape[0]
    width = (CONV_A_W - 1) * C_A
    row = lambda m: pl.BlockSpec((nb, m), lambda i: (i, 0))
    return pl.pallas_call(
        _conf_sample_kernel,
        grid=(n // nb,),
        in_specs=[row(2 * C_A), row(width), _const_spec((CONF_PAD, C_A)), _const_spec((1, C_A)),
                  _const_spec((1, C_A)), _const_spec((1, C_A))],
        out_specs=[row(C_A), row(width)],
        out_shape=[jax.ShapeDtypeStruct((n, C_A), BF16), jax.ShapeDtypeStruct((n, width), F32)],
        compiler_params=_params("parallel"),
        name="conformer_sample",
    )(a_in, state, w, b, g, bb)


def _attn_sample_kernel(n_pages, pt_ref, q_ref, bias_ref, tri_ref, sel_ref, later_ref, *refs):
    k_refs = refs[:n_pages]
    v_refs = refs[n_pages:2 * n_pages]
    o_ref = refs[2 * n_pages]
    flat = PAGE * H_B
    own = (lax.broadcasted_iota(jnp.int32, (H_B, flat), 1) % H_B
           == lax.broadcasted_iota(jnp.int32, (H_B, flat), 0))
    q = q_ref[0]
    rows = []
    for pg in range(n_pages):
        s_all = _dot_nt(q, k_refs[pg][0, 0].astype(BF16))
        rows.append(jnp.sum(jnp.where(own, s_all, 0.0), axis=0, keepdims=True))
    s = jnp.concatenate(rows, axis=0) + bias_ref[...]
    cs = _split_dot(_softplus(s), tri_ref[...])
    page_tot = _dot_exact(cs[:, 0:LANES], sel_ref[...])
    cs = cs + _dot_exact(later_ref[...], page_tot)
    w = jnp.exp(s - cs)
    out = jnp.zeros((H_B, DH_B), F32)
    for pg in range(n_pages):
        w_pg = jnp.where(own, jnp.broadcast_to(w[pg:pg + 1, :], (H_B, flat)), 0.0)
        out = out + _dot(w_pg.astype(BF16), v_refs[pg][0, 0].astype(BF16))
    o_ref[0] = out.astype(BF16)


def _attn_sample(page_table, q, bias, cache_k, cache_v, layer):
    n_seq, n_pages = page_table.shape
    flat = PAGE * H_B
    idx = jnp.arange(flat)
    same_head = (idx[:, None] % H_B) == (idx[None, :] % H_B)
    tri = (same_head & (idx[:, None] // H_B >= idx[None, :] // H_B)).astype(BF16)
    sel = (jnp.arange(LANES)[:, None] == (idx[None, :] % H_B)).astype(F32)
    later = (jnp.arange(n_pages)[None, :] > jnp.arange(n_pages)[:, None]).astype(F32)
    bias_flat = jnp.tile(bias, PAGE)[None]
    page_specs = lambda: [pl.BlockSpec((1, 1, flat, DH_B),
                                       lambda s, pt, pg=pg: (layer, pt[s * n_pages + pg], 0, 0))
                          for pg in range(n_pages)]
    const = lambda shape: pl.BlockSpec(shape, lambda s, pt: (0,) * len(shape))
    grid_spec = pltpu.PrefetchScalarGridSpec(
        num_scalar_prefetch=1,
        grid=(n_seq,),
        in_specs=[pl.BlockSpec((1, H_B, DH_B), lambda s, pt: (s, 0, 0)),
                  const((1, flat)), const((flat, flat)), const((LANES, flat)),
                  const((n_pages, n_pages))] + page_specs() + page_specs(),
        out_specs=pl.BlockSpec((1, H_B, DH_B), lambda s, pt: (s, 0, 0)),
    )
    out = pl.pallas_call(
        functools.partial(_attn_sample_kernel, n_pages),
        grid_spec=grid_spec,
        out_shape=jax.ShapeDtypeStruct((n_seq, H_B, DH_B), BF16),
        compiler_params=_params("arbitrary"),
        name="attn_sample",
    )(page_table.reshape(-1), q.reshape(n_seq, H_B, DH_B), bias_flat, tri, sel, later,
      *([cache_k] * n_pages), *([cache_v] * n_pages))
    return out.reshape(n_seq, D_B)


def _ssd_sample_prep_kernel(x_ref, st_ref, dt_ref, cw_ref, cb_ref, dtb_ref, alog_ref, e_ref,
                            nst_ref, xs_ref, xdt_t_ref, dec_t_ref, b_ref, c_ref):
    x = x_ref[...]
    hist = CONV_C_W - 1
    conv = cb_ref[...] + cw_ref[hist:hist + 1, :] * x
    for k in range(hist):
        conv = conv + cw_ref[k:k + 1, :] * st_ref[:, k * CONV_C_DIM:(k + 1) * CONV_C_DIM]
    nst_ref[:, 0:(hist - 1) * CONV_C_DIM] = st_ref[:, CONV_C_DIM:hist * CONV_C_DIM]
    nst_ref[:, (hist - 1) * CONV_C_DIM:] = x
    xbc = _silu(conv)
    xs = xbc[:, :D_INNER]
    xs_ref[...] = xs
    b_ref[...] = xbc[:, D_INNER:D_INNER + G_C * N_C]
    c_ref[...] = xbc[:, D_INNER + G_C * N_C:]
    dt = _softplus(dt_ref[...] + dtb_ref[...])
    e = e_ref[...]
    xdt_t_ref[...] = (xs * _dot_exact(dt, e)).T
    dec_t_ref[...] = jnp.exp(_dot_exact(dt * (-jnp.exp(alog_ref[...])), e)).T


def _ssd_sample_prep(xbc, state, dt, cw, cb, dtb, alog, e):
    n = xbc.shape[0]
    width = (CONV_C_W - 1) * CONV_C_DIM
    shapes = [((n, width), F32), ((n, D_INNER), F32), ((D_INNER, n), F32), ((D_INNER, n), F32),
              ((n, G_C * N_C), F32), ((n, G_C * N_C), F32)]
    full = lambda s: pl.BlockSpec(s, lambda i: (0,) * len(s))
    return pl.pallas_call(
        _ssd_sample_prep_kernel,
        grid=(1,),
        in_specs=[full(xbc.shape), full(state.shape), full(dt.shape), full(cw.shape), full(cb.shape),
                  full(dtb.shape), full(alog.shape), full(e.shape)],
        out_specs=[full(s) for s, _ in shapes],
        out_shape=[jax.ShapeDtypeStruct(s, d) for s, d in shapes],
        compiler_params=_params("arbitrary"),
        name="ssd_sample_prep",
    )(xbc, state, dt, cw, cb, dtb, alog, e)


def _ssd_sample_state_kernel(xdt_t_ref, dec_t_ref, b_ref, c_ref, h_ref, nh_ref, y_t_ref):
    j = pl.program_id(0)
    n = xdt_t_ref.shape[1]
    rows = D_INNER // G_C

    @pl.when(j == 0)
    def _():
        y_t_ref[...] = jnp.zeros_like(y_t_ref)

    mine = lax.broadcasted_iota(jnp.int32, (rows, n), 1) == j
    for g in range(G_C):
        r0 = g * rows
        pick = lambda ref: jnp.sum(jnp.where(mine, ref[r0:r0 + rows, :], 0.0), axis=1, keepdims=True)
        h_new = (pick(dec_t_ref) * h_ref[0, r0:r0 + rows, :]
                 + pick(xdt_t_ref) * b_ref[0, :, g * N_C:(g + 1) * N_C])
        nh_ref[0, r0:r0 + rows, :] = h_new
        y = jnp.sum(h_new * c_ref[0, :, g * N_C:(g + 1) * N_C], axis=1, keepdims=True)
        y_t_ref[r0:r0 + rows, :] = jnp.where(mine, y, y_t_ref[r0:r0 + rows, :])


def _ssd_sample_state(xdt_t, dec_t, bm, cm, state):
    n = state.shape[0]
    full = lambda s: pl.BlockSpec(s, lambda j: (0,) * len(s))
    per_seq = lambda s: pl.BlockSpec((1,) + s, lambda j: (j, 0, 0))
    return pl.pallas_call(
        _ssd_sample_state_kernel,
        grid=(n,),
        in_specs=[full(xdt_t.shape), full(dec_t.shape), per_seq((1, G_C * N_C)),
                  per_seq((1, G_C * N_C)), per_seq((D_INNER, N_C))],
        out_specs=[per_seq((D_INNER, N_C)), full((D_INNER, n))],
        out_shape=[jax.ShapeDtypeStruct(state.shape, F32), jax.ShapeDtypeStruct((D_INNER, n), F32)],
        compiler_params=_params("arbitrary"),
        name="ssd_sample_state",
    )(xdt_t, dec_t, bm.reshape(n, 1, -1), cm.reshape(n, 1, -1), state)


def _ssd_sample_out_kernel(y_t_ref, xs_ref, z_ref, dskip_ref, g_ref, c_ref):
    c_ref[...] = _gated_group_norm(y_t_ref[...].T, xs_ref[...], z_ref[...], dskip_ref[...],
                                   g_ref[...]).astype(BF16)


def _ssd_sample_out(y_t, xs, z, dskip, g):
    n = xs.shape[0]
    full = lambda s: pl.BlockSpec(s, lambda i: (0,) * len(s))
    return pl.pallas_call(
        _ssd_sample_out_kernel,
        grid=(1,),
        in_specs=[full(y_t.shape), full(xs.shape), full(z.shape), full(dskip.shape), full(g.shape)],
        out_specs=full((n, D_INNER)),
        out_shape=jax.ShapeDtypeStruct((n, D_INNER), BF16),
        compiler_params=_params("arbitrary"),
        name="ssd_sample_out",
    )(y_t, xs, z, dskip, g)


def _token_tile(t):
    return 256 if t % 256 == 0 else t


def _tri_keys_after(n):
    i = jnp.arange(n)
    return (i[:, None] >= i[None, :]).astype(BF16)


def kernel(x_prompt, x_sample, cache_k, cache_v, state_conv_a, state_conv_ssm, state_ssm, page_table,
           p_prompt, p_sample, g_mix, w_in, conv_a_w, conv_a_b, ln_a_g, ln_a_b, g_q, g_k, sb_bias,
           conv_ssm_w, conv_ssm_b, dt_bias, a_log, d_skip, g_ssm, w_out, g_ffn, w_ffn_gate, w_ffn_up,
           w_ffn_down, g_ple, w_ple_gate, w_ple):
    depth = w_in.shape[0]
    batch, seq, _ = x_prompt.shape
    n_seq = x_sample.shape[0]
    tp = batch * seq
    assert x_sample.shape[1] == 1
    assert seq % CONF_TILE == 0 and seq % ATT_TILE == 0 and seq % SSD_CHUNK == 0

    pad_lanes = lambda v: jnp.pad(v, ((0, 0), (0, LANES - v.shape[1])))
    seg = (jnp.arange(D_B)[:, None] // DH_B == jnp.arange(D_B)[None, :] // DH_B).astype(BF16)
    head_lanes = (jnp.arange(LANES)[:, None] == jnp.arange(D_INNER)[None, :] // P_C).astype(F32)
    tril = (jnp.arange(SSD_CHUNK)[:, None] >= jnp.arange(SSD_CHUNK)[None, :]).astype(F32)
    tri_att = _tri_keys_after(ATT_TILE)

    n_pool = cache_k.shape[1]
    cache_k = cache_k.reshape(depth, n_pool, PAGE * H_B, DH_B)
    cache_v = cache_v.reshape(depth, n_pool, PAGE * H_B, DH_B)
    conv_a_flat = state_conv_a.reshape(depth, n_seq, -1)
    conv_c_flat = state_conv_ssm.reshape(depth, n_seq, -1)
    ssm_flat = state_ssm.reshape(depth, n_seq, D_INNER, N_C)

    h_p = x_prompt.reshape(tp, D_MODEL)
    h_s = x_sample.reshape(n_seq, D_MODEL)
    outs = {name: [] for name in ("kp", "vp", "cap", "ccp", "ssp", "ks", "vs", "cas", "ccs", "sss")}
    for i in range(depth):
        w_in_b = jnp.pad(w_in[i], ((0, 0), (0, W_IN_COLS - w_in.shape[2]))).astype(BF16)
        g_mix_i = g_mix[i][None]
        gq = jnp.tile(g_q[i], H_B)[None]
        gk = jnp.tile(g_k[i], H_B)[None]
        conv_a_w_i = jnp.pad(conv_a_w[i], ((0, CONF_PAD - CONV_A_W), (0, 0)))
        conv_c_w_i = jnp.pad(conv_ssm_w[i], ((0, SUBLANES - CONV_C_W), (0, 0)))
        dtb = pad_lanes(dt_bias[i][None])
        alog = pad_lanes(a_log[i][None])
        dskip = jnp.repeat(d_skip[i], P_C)[None]
        post_w = (w_out[i].astype(BF16), g_ffn[i][None], w_ffn_gate[i].astype(BF16),
                  w_ffn_up[i].astype(BF16), w_ffn_down[i].astype(BF16), g_ple[i][None],
                  w_ple_gate[i].astype(BF16), w_ple[i].astype(BF16))

        a_in, q, k, v, kb, vb, z, xbc, dt = _in_proj(h_p, g_mix_i, w_in_b, gq, gk, seg, _token_tile(tp))
        a_out, a_tail = _conf_prompt(a_in, conv_a_w_i, conv_a_b[i][None], ln_a_g[i][None],
                                     ln_a_b[i][None], batch)
        b_out = _attn_prompt(sb_bias[i], q, kb, vb, tri_att, batch)
        c_out, ssm_p = _ssd_prompt(xbc, z, dt, conv_c_w_i, conv_ssm_b[i][None], dtb, alog, dskip,
                                   g_ssm[i][None], tril, head_lanes, batch)
        h_p = _post(h_p, a_out, b_out, c_out, p_prompt[i].reshape(tp, D_PLE), *post_w, _token_tile(tp))
        outs["kp"].append(k.reshape(batch, seq, H_B, DH_B))
        outs["vp"].append(v.reshape(batch, seq, H_B, DH_B))
        outs["cap"].append(a_tail[:, CONF_PAD - (CONV_A_W - 1):])
        outs["ccp"].append(xbc.reshape(batch, seq, CONV_C_DIM)[:, seq - (CONV_C_W - 1):])
        outs["ssp"].append(ssm_p.reshape(batch, H_C, P_C, N_C))

        a_in, q, k, v, _, _, z, xbc, dt = _in_proj(h_s, g_mix_i, w_in_b, gq, gk, seg, _token_tile(n_seq))
        a_out, conv_a_new = _conf_sample(a_in, conv_a_flat[i], conv_a_w_i, conv_a_b[i][None],
                                         ln_a_g[i][None], ln_a_b[i][None], min(n_seq, 32))
        b_out = _attn_sample(page_table, q, sb_bias[i], cache_k, cache_v, i)
        conv_c_new, xs, xdt_t, dec_t, bm, cm = _ssd_sample_prep(
            xbc, conv_c_flat[i], dt, conv_c_w_i, conv_ssm_b[i][None], dtb, alog, head_lanes)
        ssm_s, y_t = _ssd_sample_state(xdt_t, dec_t, bm, cm, ssm_flat[i])
        c_out = _ssd_sample_out(y_t, xs, z, dskip, g_ssm[i][None])
        h_s = _post(h_s, a_out, b_out, c_out, p_sample[i].reshape(n_seq, D_PLE), *post_w,
                    _token_tile(n_seq))
        outs["ks"].append(k.reshape(n_seq, 1, H_B, DH_B))
        outs["vs"].append(v.reshape(n_seq, 1, H_B, DH_B))
        outs["cas"].append(conv_a_new.reshape(n_seq, CONV_A_W - 1, C_A))
        outs["ccs"].append(conv_c_new.reshape(n_seq, CONV_C_W - 1, CONV_C_DIM))
        outs["sss"].append(ssm_s.reshape(n_seq, H_C, P_C, N_C))

    st = lambda name: jnp.stack(outs[name])
    return (h_p.reshape(batch, seq, D_MODEL), h_s.reshape(n_seq, 1, D_MODEL),
            st("kp"), st("vp"), st("cap"), st("ccp"), st("ssp"),
            st("ks"), st("vs"), st("cas"), st("ccs"), st("sss"))
```

```python
import functools
import math

import jax
import jax.numpy as jnp
from jax import lax
from jax.experimental import pallas as pl
from jax.experimental.pallas import tpu as pltpu

F32 = jnp.float32
BF16 = jnp.bfloat16
HIGHEST = lax.Precision.HIGHEST

D_MODEL = 1024
C_A = 512
CONV_A_W = 31
H_B = 8
DH_B = 64
D_B = H_B * DH_B
D_INNER = 1024
H_C = 16
P_C = 64
G_C = 2
N_C = 128
CONV_C_W = 4
CONV_C_DIM = D_INNER + 2 * G_C * N_C
D_FF = 2816
D_PLE = 256
EPS = 1e-6
PAGE = 128

LANES = 128
SUBLANES = 8
VMEM_LIMIT = 56 * 1024 * 1024

OFF_A = 0
OFF_Q = 2 * C_A
OFF_K = OFF_Q + D_B
OFF_V = OFF_K + D_B
OFF_Z = OFF_V + D_B
OFF_X = OFF_Z + D_INNER
OFF_DT = OFF_X + CONV_C_DIM
W_IN_COLS = OFF_DT + LANES

FF_CHUNK = 256
SSD_CHUNK = 128
CONF_TILE = 512
CONF_ROWS = 32
CONF_PAD = 32
ATT_TILE = 256
HEADS_PER_STEP = LANES // DH_B
LOG2E = 1.4426950408889634


def _const_spec(shape):
    zeros = (0,) * len(shape)
    return pl.BlockSpec(shape, lambda *_: zeros, pipeline_mode=pl.Buffered(1))


def _params(*sem):
    return pltpu.CompilerParams(dimension_semantics=sem, vmem_limit_bytes=VMEM_LIMIT)


def _sigmoid(x):
    return 1.0 / (1.0 + jnp.exp(-x))


def _silu(x):
    return x * _sigmoid(x)


def _softplus(x):
    return jnp.maximum(x, 0.0) + jnp.log(1.0 + jnp.exp(-jnp.abs(x)))


def _softplus2(x):
    sign = jnp.uint32(0x80000000)
    neg_abs = lax.bitcast_convert_type(lax.bitcast_convert_type(x, jnp.uint32) | sign, F32)
    return jnp.maximum(x, 0.0) + jnp.log(1.0 + jnp.exp2(neg_abs)) * LOG2E


def _rms(x, g):
    return x * lax.rsqrt(jnp.mean(x * x, axis=-1, keepdims=True) + EPS) * g


def _dot(a, b):
    return jnp.dot(a, b, preferred_element_type=F32)


def _dot_nt(a, b):
    return lax.dot_general(a, b, (((1,), (1,)), ((), ())), preferred_element_type=F32)


def _dot_exact(a, b):
    return jnp.dot(a, b, precision=HIGHEST, preferred_element_type=F32)


def _split_dot(x, m):
    hi = x.astype(BF16)
    lo = (x - hi.astype(F32)).astype(BF16)
    return _dot(hi, m) + _dot(lo, m)


def _in_proj_kernel(h_ref, g_ref, w_ref, gq_ref, gk_ref, seg_ref,
                    a_ref, q_ref, k_ref, v_ref, kb_ref, vb_ref, z_ref, x_ref, dt_ref):
    u = _rms(h_ref[...], g_ref[...]).astype(BF16)

    def proj(lo, hi):
        return _dot(u, w_ref[:, lo:hi])

    a_ref[...] = proj(OFF_A, OFF_Q)
    z_ref[...] = proj(OFF_Z, OFF_X)
    x_ref[...] = proj(OFF_X, OFF_DT)
    dt_ref[...] = proj(OFF_DT, W_IN_COLS)

    seg = seg_ref[...]

    def head_norm(t, g):
        ms = _split_dot(t * t, seg) * (1.0 / DH_B)
        return t * lax.rsqrt(ms + EPS) * g

    qn = head_norm(proj(OFF_Q, OFF_K), gq_ref[...])
    kn = head_norm(proj(OFF_K, OFF_V), gk_ref[...])
    v = proj(OFF_V, OFF_Z)
    q_ref[...] = (qn * (DH_B ** -0.5 * LOG2E)).astype(BF16)
    k_ref[0] = kn.T
    v_ref[0] = v.T
    kb_ref[...] = kn.astype(BF16)
    vb_ref[...] = v.astype(BF16)


def _in_proj(h, g_mix, w_in, gq, gk, seg, tm, batch):
    t = h.shape[0]
    seq = t // batch
    nt = seq // tm
    row = lambda n: pl.BlockSpec((tm, n), lambda i: (i, 0))
    col = pl.BlockSpec((1, D_B, tm), lambda i: (i // nt, 0, i % nt))
    rows = lambda n, d: (row(n), jax.ShapeDtypeStruct((t, n), d))
    cols = (col, jax.ShapeDtypeStruct((batch, D_B, seq), F32))
    outs = (rows(2 * C_A, F32), rows(D_B, BF16), cols, cols, rows(D_B, BF16), rows(D_B, BF16),
            rows(D_INNER, F32), rows(CONV_C_DIM, F32), rows(LANES, F32))
    return pl.pallas_call(
        _in_proj_kernel,
        grid=(t // tm,),
        in_specs=[row(D_MODEL), _const_spec((1, D_MODEL)), _const_spec((D_MODEL, W_IN_COLS)),
                  _const_spec((1, D_B)), _const_spec((1, D_B)), _const_spec((D_B, D_B))],
        out_specs=[spec for spec, _ in outs],
        out_shape=[shape for _, shape in outs],
        compiler_params=_params("parallel"),
        name="in_proj",
    )(h, g_mix, w_in, gq, gk, seg)


def _layer_norm_silu(x, g, b):
    mu = jnp.mean(x, axis=-1, keepdims=True)
    xc = x - mu
    y = xc * lax.rsqrt(jnp.mean(xc * xc, axis=-1, keepdims=True) + EPS)
    return _silu(y * g + b)


def _conf_prompt_kernel(a_ref, w_ref, b_ref, g_ref, bb_ref, o_ref, tail_ref, buf):
    tile = a_ref.shape[0]

    @pl.when(pl.program_id(1) == 0)
    def _():
        buf[0:CONF_PAD, :] = jnp.zeros((CONF_PAD, C_A), F32)

    a = a_ref[...]
    buf[CONF_PAD:CONF_PAD + tile, :] = a[:, :C_A] * _sigmoid(a[:, C_A:])
    first = CONF_PAD - (CONV_A_W - 1)
    for r0 in range(0, tile, CONF_ROWS):
        acc = jnp.zeros((CONF_ROWS, C_A), F32)
        for k in range(CONV_A_W):
            acc = acc + w_ref[k:k + 1, :] * buf[first + r0 + k:first + r0 + k + CONF_ROWS, :]
        y = _layer_norm_silu(acc + b_ref[...], g_ref[...], bb_ref[...])
        o_ref[r0:r0 + CONF_ROWS, :] = y.astype(BF16)
    tail = buf[tile:tile + CONF_PAD, :]
    buf[0:CONF_PAD, :] = tail
    tail_ref[0] = tail


def _conf_prompt(a_in, w, b, g, bb, batch):
    t = a_in.shape[0]
    nt = t // batch // CONF_TILE
    return pl.pallas_call(
        _conf_prompt_kernel,
        grid=(batch, nt),
        in_specs=[pl.BlockSpec((CONF_TILE, 2 * C_A), lambda bi, ti: (bi * nt + ti, 0)),
                  _const_spec((CONF_PAD, C_A)), _const_spec((1, C_A)), _const_spec((1, C_A)),
                  _const_spec((1, C_A))],
        out_specs=[pl.BlockSpec((CONF_TILE, C_A), lambda bi, ti: (bi * nt + ti, 0)),
                   pl.BlockSpec((1, CONF_PAD, C_A), lambda bi, ti: (bi, 0, 0))],
        out_shape=[jax.ShapeDtypeStruct((t, C_A), BF16),
                   jax.ShapeDtypeStruct((batch, CONF_PAD, C_A), F32)],
        scratch_shapes=[pltpu.VMEM((CONF_PAD + CONF_TILE, C_A), F32)],
        compiler_params=_params("parallel", "arbitrary"),
        name="conformer_prompt",
    )(a_in, w, b, g, bb)


def _attn_prompt_kernel(bias_ref, q_ref, k_ref, v_ref, tri_ref, o_ref, s_ref, cs_ref, acc_ref, carry_ref):
    qi = pl.program_id(1)
    tile = q_ref.shape[0]
    tri = tri_ref[...]
    lane = lax.broadcasted_iota(jnp.int32, (tile, LANES), 1)
    row = lax.broadcasted_iota(jnp.int32, (tile, tile), 0)
    col = lax.broadcasted_iota(jnp.int32, (tile, tile), 1)
    causal = col < row

    heads = []
    for lt in range(D_B // LANES):
        q2 = q_ref[:, lt * LANES:(lt + 1) * LANES].astype(F32)
        for hh in range(HEADS_PER_STEP):
            qh = jnp.where((lane // DH_B) == hh, q2, 0.0).astype(BF16)
            heads.append((lt, qh, bias_ref[lt * HEADS_PER_STEP + hh]))
    n = len(heads)

    def scores(i, start):
        lt, qh, bias = heads[i]
        s_ref[i] = _dot_nt(qh, k_ref[pl.ds(start, tile), lt * LANES:(lt + 1) * LANES]) + bias

    def later_sums(i, masked):
        sp = _softplus2(s_ref[i])
        if masked:
            sp = jnp.where(causal, sp, 0.0)
        cs_ref[i] = _dot(sp.astype(BF16), tri)

    def weigh(i, start, masked):
        lt = heads[i][0]
        cs = cs_ref[i]
        carry = carry_ref[i]
        w = jnp.exp2(s_ref[i] - cs - carry)
        if masked:
            w = jnp.where(causal, w, 0.0)
        acc_ref[i] += _dot(w.astype(BF16), v_ref[pl.ds(start, tile), lt * LANES:(lt + 1) * LANES])
        carry_ref[i] = carry + cs[:, 0:1]

    acc_ref[...] = jnp.zeros_like(acc_ref)
    carry_ref[...] = jnp.zeros_like(carry_ref)
    start0 = pl.multiple_of(qi * tile, tile)
    for i in range(n):
        scores(i, start0)
    for i in range(n):
        later_sums(i, True)
    nxt0 = pl.multiple_of(jnp.maximum(qi - 1, 0) * tile, tile)
    for i in range(n):
        weigh(i, start0, True)
        scores(i, nxt0)
    for i in range(n):
        later_sums(i, False)

    @pl.loop(0, qi)
    def _(j):
        cur = qi - 1 - j
        start = pl.multiple_of(cur * tile, tile)
        nxt = pl.multiple_of(jnp.maximum(cur - 1, 0) * tile, tile)
        for i in range(n):
            weigh(i, start, False)
            scores(i, nxt)
        for i in range(n):
            later_sums(i, False)

    for lt in range(D_B // LANES):
        out = acc_ref[lt * HEADS_PER_STEP]
        for hh in range(1, HEADS_PER_STEP):
            out = jnp.where((lane // DH_B) == hh, acc_ref[lt * HEADS_PER_STEP + hh], out)
        o_ref[:, lt * LANES:(lt + 1) * LANES] = out.astype(BF16)


def _attn_prompt(bias, q, k, v, tri, batch):
    t = q.shape[0]
    seq = t // batch
    nq = seq // ATT_TILE
    resident = lambda: pl.BlockSpec((seq, D_B), lambda b, i, *_: (b, 0), pipeline_mode=pl.Buffered(1))
    grid_spec = pltpu.PrefetchScalarGridSpec(
        num_scalar_prefetch=1,
        grid=(batch, nq),
        in_specs=[pl.BlockSpec((ATT_TILE, D_B), lambda b, i, *_: (b * nq + i, 0)),
                  resident(), resident(),
                  pl.BlockSpec((ATT_TILE, ATT_TILE), lambda b, i, *_: (0, 0))],
        out_specs=pl.BlockSpec((ATT_TILE, D_B), lambda b, i, *_: (b * nq + i, 0)),
        scratch_shapes=[pltpu.VMEM((H_B, ATT_TILE, ATT_TILE), F32),
                        pltpu.VMEM((H_B, ATT_TILE, ATT_TILE), F32),
                        pltpu.VMEM((H_B, ATT_TILE, LANES), F32),
                        pltpu.VMEM((H_B, ATT_TILE, 1), F32)],
    )
    return pl.pallas_call(
        _attn_prompt_kernel,
        grid_spec=grid_spec,
        out_shape=jax.ShapeDtypeStruct((t, D_B), BF16),
        compiler_params=_params("parallel", "arbitrary"),
        name="attn_prompt",
    )(bias, q, k, v, tri)


def _gated_group_norm(y, xs, z, dskip, g):
    y = (y + dskip * xs) * _silu(z)
    gw = D_INNER // G_C
    return jnp.concatenate(
        [_rms(y[:, i * gw:(i + 1) * gw], g[:, i * gw:(i + 1) * gw]) for i in range(G_C)], axis=1)


def _ssd_prompt_kernel(x_ref, z_ref, dt_ref, cw_ref, cb_ref, dtb_ref, alog_ref, dskip_ref, g_ref,
                       tril_ref, e_ref, c_ref, st_ref, cbuf, ybuf):
    q = x_ref.shape[0]

    @pl.when(pl.program_id(1) == 0)
    def _():
        cbuf[0:SUBLANES, :] = jnp.zeros((SUBLANES, CONV_C_DIM), F32)
        st_ref[...] = jnp.zeros_like(st_ref)

    cbuf[SUBLANES:SUBLANES + q, :] = x_ref[...]
    first = SUBLANES - (CONV_C_W - 1)
    conv = cb_ref[...] + sum(cw_ref[k:k + 1, :] * cbuf[first + k:first + k + q, :]
                             for k in range(CONV_C_W))
    cbuf[0:SUBLANES, :] = cbuf[q:q + SUBLANES, :]
    xbc = _silu(conv)
    xs = xbc[:, :D_INNER]
    bm = xbc[:, D_INNER:D_INNER + G_C * N_C].astype(BF16)
    cm = xbc[:, D_INNER + G_C * N_C:].astype(BF16)

    dt = _softplus(dt_ref[...] + dtb_ref[...])
    a_cum = _dot_exact(tril_ref[...], dt * (-jnp.exp(alog_ref[...])))
    e = e_ref[...]
    dt_x = _dot_exact(dt, e)
    a_x = _dot_exact(a_cum, e)
    a_cum_t = a_cum.T
    a_xt = a_x.T
    xdt = xs * dt_x
    exp_a_x = jnp.exp(a_x)

    row = lax.broadcasted_iota(jnp.int32, (q, q), 0)
    col = lax.broadcasted_iota(jnp.int32, (q, q), 1)
    causal = col <= row
    lane = lax.broadcasted_iota(jnp.int32, (q, LANES), 1)
    heads_per_group = H_C // G_C

    for g in range(G_C):
        bg = bm[:, g * N_C:(g + 1) * N_C]
        cg = cm[:, g * N_C:(g + 1) * N_C]
        cb = _dot_nt(cg, bg)
        for pp in range(heads_per_group // HEADS_PER_STEP):
            pair = g * (heads_per_group // HEADS_PER_STEP) + pp
            lo = pair * LANES
            xdt_p = xdt[:, lo:lo + LANES]
            xdt_pb = xdt_p.astype(BF16)
            y_p = None
            for hh in range(HEADS_PER_STEP):
                h = pair * HEADS_PER_STEP + hh
                seg = a_cum[:, h:h + 1] - a_cum_t[h:h + 1, :]
                m = (cb * jnp.exp(jnp.where(causal, seg, -jnp.inf))).astype(BF16)
                r = _dot(m, xdt_pb)
                y_p = r if y_p is None else jnp.where((lane // P_C) == hh, r, y_p)
            h_prev = st_ref[0, lo:lo + LANES, :]
            y_p = y_p + _dot_nt(cg, h_prev.astype(BF16)) * exp_a_x[:, lo:lo + LANES]
            ybuf[:, lo:lo + LANES] = y_p
            a_t = a_xt[lo:lo + LANES, :]
            tot = a_t[:, q - 1:q]
            s_new = _dot((xdt_p.T * jnp.exp(tot - a_t)).astype(BF16), bg)
            st_ref[0, lo:lo + LANES, :] = jnp.exp(tot) * h_prev + s_new

    c_ref[...] = _gated_group_norm(ybuf[...], xs, z_ref[...], dskip_ref[...], g_ref[...]).astype(BF16)


def _ssd_prompt(xbc, z, dt, cw, cb, dtb, alog, dskip, g, tril, e, batch):
    t = xbc.shape[0]
    nc = t // batch // SSD_CHUNK
    row = lambda n: pl.BlockSpec((SSD_CHUNK, n), lambda bi, ci: (bi * nc + ci, 0))
    return pl.pallas_call(
        _ssd_prompt_kernel,
        grid=(batch, nc),
        in_specs=[row(CONV_C_DIM), row(D_INNER), row(LANES),
                  _const_spec((SUBLANES, CONV_C_DIM)), _const_spec((1, CONV_C_DIM)),
                  _const_spec((1, LANES)), _const_spec((1, LANES)), _const_spec((1, D_INNER)),
                  _const_spec((1, D_INNER)), _const_spec((SSD_CHUNK, SSD_CHUNK)),
                  _const_spec((LANES, D_INNER))],
        out_specs=[row(D_INNER), pl.BlockSpec((1, D_INNER, N_C), lambda bi, ci: (bi, 0, 0))],
        out_shape=[jax.ShapeDtypeStruct((t, D_INNER), BF16),
                   jax.ShapeDtypeStruct((batch, D_INNER, N_C), F32)],
        scratch_shapes=[pltpu.VMEM((SUBLANES + SSD_CHUNK, CONV_C_DIM), F32),
                        pltpu.VMEM((SSD_CHUNK, D_INNER), F32)],
        compiler_params=_params("parallel", "arbitrary"),
        name="ssd_prompt",
    )(xbc, z, dt, cw, cb, dtb, alog, dskip, g, tril, e)


def _post_kernel(h_ref, a_ref, b_ref, c_ref, pe_ref, wo_ref, gf_ref, wg_ref, wu_ref, wd_ref,
                 gp_ref, wpg_ref, wp_ref, o_ref):
    h = (h_ref[...] + _dot(a_ref[...], wo_ref[0:C_A, :]) + _dot(b_ref[...], wo_ref[C_A:C_A + D_B, :])
         + _dot(c_ref[...], wo_ref[C_A + D_B:, :]))
    f = _rms(h, gf_ref[...]).astype(BF16)
    acc = jnp.zeros_like(h)
    for c0 in range(0, D_FF, FF_CHUNK):
        gate = _dot(f, wg_ref[:, c0:c0 + FF_CHUNK])
        up = _dot(f, wu_ref[:, c0:c0 + FF_CHUNK])
        acc = acc + _dot((_silu(gate) * up).astype(BF16), wd_ref[c0:c0 + FF_CHUNK, :])
    h = h + acc
    gate = _sigmoid(_dot(_rms(h, gp_ref[...]).astype(BF16), wpg_ref[...]))
    o_ref[...] = h + _dot(pe_ref[...].astype(BF16), wp_ref[...]) * gate


def _post(h, a, b, c, pe, wo, gf, wg, wu, wd, gp, wpg, wp, tm):
    t = h.shape[0]
    row = lambda n: pl.BlockSpec((tm, n), lambda i: (i, 0))
    return pl.pallas_call(
        _post_kernel,
        grid=(t // tm,),
        in_specs=[row(D_MODEL), row(C_A), row(D_B), row(D_INNER), row(D_PLE),
                  _const_spec(wo.shape), _const_spec((1, D_MODEL)), _const_spec(wg.shape),
                  _const_spec(wu.shape), _const_spec(wd.shape), _const_spec((1, D_MODEL)),
                  _const_spec(wpg.shape), _const_spec(wp.shape)],
        out_specs=row(D_MODEL),
        out_shape=jax.ShapeDtypeStruct((t, D_MODEL), F32),
        compiler_params=_params("parallel"),
        name="post",
    )(h, a, b, c, pe, wo, gf, wg, wu, wd, gp, wpg, wp)


def _conf_sample_kernel(a_ref, st_ref, w_ref, b_ref, g_ref, bb_ref, o_ref, nst_ref):
    a = a_ref[...]
    glu = a[:, :C_A] * _sigmoid(a[:, C_A:])
    hist = CONV_A_W - 1
    acc = b_ref[...] + w_ref[hist:hist + 1, :] * glu
    for k in range(hist):
        acc = acc + w_ref[k:k + 1, :] * st_ref[k]
    o_ref[...] = _layer_norm_silu(acc, g_ref[...], bb_ref[...]).astype(BF16)
    for k in range(hist - 1):
        nst_ref[k] = st_ref[k + 1]
    nst_ref[hist - 1] = glu


def _conf_sample(a_in, state, w, b, g, bb, nb):
    n = a_in.shape[0]
    hist = CONV_A_W - 1
    row = lambda m: pl.BlockSpec((nb, m), lambda i: (i, 0))
    taps = pl.BlockSpec((hist, nb, C_A), lambda i: (0, i, 0))
    return pl.pallas_call(
        _conf_sample_kernel,
        grid=(n // nb,),
        in_specs=[row(2 * C_A), taps, _const_spec((CONF_PAD, C_A)), _const_spec((1, C_A)),
                  _const_spec((1, C_A)), _const_spec((1, C_A))],
        out_specs=[row(C_A), taps],
        out_shape=[jax.ShapeDtypeStruct((n, C_A), BF16), jax.ShapeDtypeStruct((hist, n, C_A), F32)],
        compiler_params=_params("parallel"),
        name="conformer_sample",
    )(a_in, state, w, b, g, bb)


def _attn_sample_kernel(n_pages, pt_ref, q_ref, bias_ref, tri_ref, later_ref, *refs):
    k_refs = refs[:n_pages]
    v_refs = refs[n_pages:2 * n_pages]
    o_ref = refs[2 * n_pages]
    own = (lax.broadcasted_iota(jnp.int32, (H_B, D_B), 1) // DH_B
           == lax.broadcasted_iota(jnp.int32, (H_B, D_B), 0))
    q_rows = jnp.broadcast_to(q_ref[0].astype(F32), (H_B, D_B))
    qmat = jnp.where(own, q_rows, 0.0).astype(BF16)
    s = jnp.concatenate([_dot(qmat, k_refs[pg][0, 0].astype(BF16)) for pg in range(n_pages)], axis=0)
    s = s + bias_ref[...]
    cs = _split_dot(_softplus2(s), tri_ref[...])
    page_tot = jnp.broadcast_to(cs[:, 0:1], cs.shape)
    cs = cs + _dot_exact(later_ref[...], page_tot)
    w = jnp.exp2(s - cs).astype(BF16)
    acc = jnp.zeros((H_B, D_B), F32)
    for pg in range(n_pages):
        acc = acc + _dot_nt(w[pg * H_B:(pg + 1) * H_B, :], v_refs[pg][0, 0].astype(BF16))
    o_ref[0] = jnp.sum(jnp.where(own, acc, 0.0), axis=0, keepdims=True).astype(BF16)


def _attn_sample(page_table, q, bias, cache_kt, cache_vt, layer):
    n_seq, n_pages = page_table.shape
    rows = n_pages * H_B
    r = jnp.arange(rows)
    later = ((r[:, None] % H_B == r[None, :] % H_B) & (r[None, :] // H_B > r[:, None] // H_B)).astype(F32)
    bias_rows = jnp.broadcast_to(jnp.tile(bias, n_pages)[:, None], (rows, PAGE))
    page_specs = lambda: [pl.BlockSpec((1, 1, D_B, PAGE),
                                       lambda s, pt, pg=pg: (layer, pt[s * n_pages + pg], 0, 0))
                          for pg in range(n_pages)]
    const = lambda shape: pl.BlockSpec(shape, lambda s, pt: (0,) * len(shape))
    grid_spec = pltpu.PrefetchScalarGridSpec(
        num_scalar_prefetch=1,
        grid=(n_seq,),
        in_specs=[pl.BlockSpec((1, 1, D_B), lambda s, pt: (s, 0, 0)),
                  const((rows, PAGE)), const((PAGE, PAGE)), const((rows, rows))]
                 + page_specs() + page_specs(),
        out_specs=pl.BlockSpec((1, 1, D_B), lambda s, pt: (s, 0, 0)),
    )
    out = pl.pallas_call(
        functools.partial(_attn_sample_kernel, n_pages),
        grid_spec=grid_spec,
        out_shape=jax.ShapeDtypeStruct((n_seq, 1, D_B), BF16),
        compiler_params=_params("arbitrary"),
        name="attn_sample",
    )(page_table.reshape(-1), q.reshape(n_seq, 1, D_B), bias_rows, _tri_keys_after(PAGE), later,
      *([cache_kt] * n_pages), *([cache_vt] * n_pages))
    return out.reshape(n_seq, D_B)


def _ssd_sample_prep_kernel(x_ref, st_ref, dt_ref, cw_ref, cb_ref, dtb_ref, alog_ref, e_ref,
                            nst_ref, xs_ref, xdt_t_ref, dec_t_ref, b_ref, c_ref):
    x = x_ref[...]
    hist = CONV_C_W - 1
    conv = cb_ref[...] + cw_ref[hist:hist + 1, :] * x
    for k in range(hist):
        conv = conv + cw_ref[k:k + 1, :] * st_ref[k]
    for k in range(hist - 1):
        nst_ref[k] = st_ref[k + 1]
    nst_ref[hist - 1] = x
    xbc = _silu(conv)
    xs = xbc[:, :D_INNER]
    xs_ref[...] = xs
    b_ref[...] = xbc[:, D_INNER:D_INNER + G_C * N_C]
    c_ref[...] = xbc[:, D_INNER + G_C * N_C:]
    dt = _softplus(dt_ref[...] + dtb_ref[...])
    e = e_ref[...]
    xdt_t_ref[...] = (xs * _dot_exact(dt, e)).T
    dec_t_ref[...] = jnp.exp(_dot_exact(dt * (-jnp.exp(alog_ref[...])), e)).T


def _ssd_sample_prep(xbc, state, dt, cw, cb, dtb, alog, e):
    n = xbc.shape[0]
    shapes = [(state.shape, F32), ((n, D_INNER), F32), ((D_INNER, n), F32), ((D_INNER, n), F32),
              ((n, G_C * N_C), F32), ((n, G_C * N_C), F32)]
    full = lambda s: pl.BlockSpec(s, lambda i: (0,) * len(s))
    return pl.pallas_call(
        _ssd_sample_prep_kernel,
        grid=(1,),
        in_specs=[full(xbc.shape), full(state.shape), full(dt.shape), full(cw.shape), full(cb.shape),
                  full(dtb.shape), full(alog.shape), full(e.shape)],
        out_specs=[full(s) for s, _ in shapes],
        out_shape=[jax.ShapeDtypeStruct(s, d) for s, d in shapes],
        compiler_params=_params("arbitrary"),
        name="ssd_sample_prep",
    )(xbc, state, dt, cw, cb, dtb, alog, e)


def _ssd_sample_state_kernel(xdt_t_ref, dec_t_ref, b_ref, c_ref, h_ref, nh_ref, y_t_ref):
    j = pl.program_id(0)
    n = xdt_t_ref.shape[1]
    rows = D_INNER // G_C

    @pl.when(j == 0)
    def _():
        y_t_ref[...] = jnp.zeros_like(y_t_ref)

    mine = lax.broadcasted_iota(jnp.int32, (rows, n), 1) == j
    for g in range(G_C):
        r0 = g * rows
        pick = lambda ref: jnp.sum(jnp.where(mine, ref[r0:r0 + rows, :], 0.0), axis=1, keepdims=True)
        h_new = (pick(dec_t_ref) * h_ref[0, r0:r0 + rows, :]
                 + pick(xdt_t_ref) * b_ref[0, :, g * N_C:(g + 1) * N_C])
        nh_ref[0, r0:r0 + rows, :] = h_new
        y = jnp.sum(h_new * c_ref[0, :, g * N_C:(g + 1) * N_C], axis=1, keepdims=True)
        y_t_ref[r0:r0 + rows, :] = jnp.where(mine, y, y_t_ref[r0:r0 + rows, :])


def _ssd_sample_state(xdt_t, dec_t, bm, cm, state):
    n = state.shape[0]
    full = lambda s: pl.BlockSpec(s, lambda j: (0,) * len(s))
    per_seq = lambda s: pl.BlockSpec((1,) + s, lambda j: (j, 0, 0))
    return pl.pallas_call(
        _ssd_sample_state_kernel,
        grid=(n,),
        in_specs=[full(xdt_t.shape), full(dec_t.shape), per_seq((1, G_C * N_C)),
                  per_seq((1, G_C * N_C)), per_seq((D_INNER, N_C))],
        out_specs=[per_seq((D_INNER, N_C)), full((D_INNER, n))],
        out_shape=[jax.ShapeDtypeStruct(state.shape, F32), jax.ShapeDtypeStruct((D_INNER, n), F32)],
        compiler_params=_params("arbitrary"),
        name="ssd_sample_state",
    )(xdt_t, dec_t, bm.reshape(n, 1, -1), cm.reshape(n, 1, -1), state)


def _ssd_sample_out_kernel(y_t_ref, xs_ref, z_ref, dskip_ref, g_ref, c_ref):
    c_ref[...] = _gated_group_norm(y_t_ref[...].T, xs_ref[...], z_ref[...], dskip_ref[...],
                                   g_ref[...]).astype(BF16)


def _ssd_sample_out(y_t, xs, z, dskip, g):
    n = xs.shape[0]
    full = lambda s: pl.BlockSpec(s, lambda i: (0,) * len(s))
    return pl.pallas_call(
        _ssd_sample_out_kernel,
        grid=(1,),
        in_specs=[full(y_t.shape), full(xs.shape), full(z.shape), full(dskip.shape), full(g.shape)],
        out_specs=full((n, D_INNER)),
        out_shape=jax.ShapeDtypeStruct((n, D_INNER), BF16),
        compiler_params=_params("arbitrary"),
        name="ssd_sample_out",
    )(y_t, xs, z, dskip, g)


def _token_tile(t):
    return 256 if t % 256 == 0 else t


def _tri_keys_after(n):
    i = jnp.arange(n)
    return (i[:, None] >= i[None, :]).astype(BF16)


def kernel(x_prompt, x_sample, cache_k, cache_v, state_conv_a, state_conv_ssm, state_ssm, page_table,
           p_prompt, p_sample, g_mix, w_in, conv_a_w, conv_a_b, ln_a_g, ln_a_b, g_q, g_k, sb_bias,
           conv_ssm_w, conv_ssm_b, dt_bias, a_log, d_skip, g_ssm, w_out, g_ffn, w_ffn_gate, w_ffn_up,
           w_ffn_down, g_ple, w_ple_gate, w_ple):
    depth = w_in.shape[0]
    batch, seq, _ = x_prompt.shape
    n_seq = x_sample.shape[0]
    tp = batch * seq
    assert x_sample.shape[1] == 1
    assert seq % CONF_TILE == 0 and seq % ATT_TILE == 0 and seq % SSD_CHUNK == 0

    pad_lanes = lambda v: jnp.pad(v, ((0, 0), (0, LANES - v.shape[1])))
    seg = (jnp.arange(D_B)[:, None] // DH_B == jnp.arange(D_B)[None, :] // DH_B).astype(BF16)
    head_lanes = (jnp.arange(LANES)[:, None] == jnp.arange(D_INNER)[None, :] // P_C).astype(F32)
    tril = (jnp.arange(SSD_CHUNK)[:, None] >= jnp.arange(SSD_CHUNK)[None, :]).astype(F32)
    tri_att = _tri_keys_after(ATT_TILE)

    n_pool = cache_k.shape[1]
    cache_kt = jnp.transpose(cache_k, (0, 1, 3, 4, 2)).reshape(depth, n_pool, D_B, PAGE)
    cache_vt = jnp.transpose(cache_v, (0, 1, 3, 4, 2)).reshape(depth, n_pool, D_B, PAGE)
    conv_a_taps = jnp.transpose(state_conv_a, (0, 2, 1, 3))
    conv_c_taps = jnp.transpose(state_conv_ssm, (0, 2, 1, 3))
    ssm_flat = state_ssm.reshape(depth, n_seq, D_INNER, N_C)

    h_p = x_prompt.reshape(tp, D_MODEL)
    h_s = x_sample.reshape(n_seq, D_MODEL)
    outs = {name: [] for name in ("kp", "vp", "cap", "ccp", "ssp", "ks", "vs", "cas", "ccs", "sss")}
    for i in range(depth):
        w_in_b = jnp.pad(w_in[i], ((0, 0), (0, W_IN_COLS - w_in.shape[2]))).astype(BF16)
        g_mix_i = g_mix[i][None]
        gq = jnp.tile(g_q[i], H_B)[None]
        gk = jnp.tile(g_k[i], H_B)[None]
        conv_a_w_i = jnp.pad(conv_a_w[i], ((0, CONF_PAD - CONV_A_W), (0, 0)))
        conv_c_w_i = jnp.pad(conv_ssm_w[i], ((0, SUBLANES - CONV_C_W), (0, 0)))
        dtb = pad_lanes(dt_bias[i][None])
        alog = pad_lanes(a_log[i][None])
        dskip = jnp.repeat(d_skip[i], P_C)[None]
        post_w = (w_out[i].astype(BF16), g_ffn[i][None], w_ffn_gate[i].astype(BF16),
                  w_ffn_up[i].astype(BF16), w_ffn_down[i].astype(BF16), g_ple[i][None],
                  w_ple_gate[i].astype(BF16), w_ple[i].astype(BF16))

        a_in, q, kt, vt, kb, vb, z, xbc, dt = _in_proj(h_p, g_mix_i, w_in_b, gq, gk, seg,
                                                       _token_tile(seq), batch)
        a_out, a_tail = _conf_prompt(a_in, conv_a_w_i, conv_a_b[i][None], ln_a_g[i][None],
                                     ln_a_b[i][None], batch)
        bias2 = sb_bias[i] * LOG2E
        b_out = _attn_prompt(bias2, q, kb, vb, tri_att, batch)
        c_out, ssm_p = _ssd_prompt(xbc, z, dt, conv_c_w_i, conv_ssm_b[i][None], dtb, alog, dskip,
                                   g_ssm[i][None], tril, head_lanes, batch)
        h_p = _post(h_p, a_out, b_out, c_out, p_prompt[i].reshape(tp, D_PLE), *post_w, _token_tile(tp))
        heads_last = lambda x: jnp.transpose(x.reshape(x.shape[0], H_B, DH_B, x.shape[2]), (0, 3, 1, 2))
        outs["kp"].append(heads_last(kt))
        outs["vp"].append(heads_last(vt))
        outs["cap"].append(a_tail[:, CONF_PAD - (CONV_A_W - 1):])
        outs["ccp"].append(xbc.reshape(batch, seq, CONV_C_DIM)[:, seq - (CONV_C_W - 1):])
        outs["ssp"].append(ssm_p.reshape(batch, H_C, P_C, N_C))

        a_in, q, kt, vt, _, _, z, xbc, dt = _in_proj(h_s, g_mix_i, w_in_b, gq, gk, seg,
                                                     _token_tile(n_seq), 1)
        a_out, conv_a_new = _conf_sample(a_in, conv_a_taps[i], conv_a_w_i, conv_a_b[i][None],
                                         ln_a_g[i][None], ln_a_b[i][None], min(n_seq, 32))
        b_out = _attn_sample(page_table, q, bias2, cache_kt, cache_vt, i)
        conv_c_new, xs, xdt_t, dec_t, bm, cm = _ssd_sample_prep(
            xbc, conv_c_taps[i], dt, conv_c_w_i, conv_ssm_b[i][None], dtb, alog, head_lanes)
        ssm_s, y_t = _ssd_sample_state(xdt_t, dec_t, bm, cm, ssm_flat[i])
        c_out = _ssd_sample_out(y_t, xs, z, dskip, g_ssm[i][None])
        h_s = _post(h_s, a_out, b_out, c_out, p_sample[i].reshape(n_seq, D_PLE), *post_w,
                    _token_tile(n_seq))
        outs["ks"].append(jnp.transpose(heads_last(kt), (1, 0, 2, 3)))
        outs["vs"].append(jnp.transpose(heads_last(vt), (1, 0, 2, 3)))
        outs["cas"].append(jnp.transpose(conv_a_new, (1, 0, 2)))
        outs["ccs"].append(jnp.transpose(conv_c_new, (1, 0, 2)))
        outs["sss"].append(ssm_s.reshape(n_seq, H_C, P_C, N_C))

    st = lambda name: jnp.stack(outs[name])
    return (h_p.reshape(batch, seq, D_MODEL), h_s.reshape(n_seq, 1, D_MODEL),
            st("kp"), st("vp"), st("cap"), st("ccp"), st("ssp"),
            st("ks"), st("vs"), st("cas"), st("ccs"), st("sss"))
```

```python
import functools
import math

import jax
import jax.numpy as jnp
from jax import lax
from jax.experimental import pallas as pl
from jax.experimental.pallas import tpu as pltpu

F32 = jnp.float32
BF16 = jnp.bfloat16
HIGHEST = lax.Precision.HIGHEST

D_MODEL = 1024
C_A = 512
CONV_A_W = 31
H_B = 8
DH_B = 64
D_B = H_B * DH_B
D_INNER = 1024
H_C = 16
P_C = 64
G_C = 2
N_C = 128
CONV_C_W = 4
CONV_C_DIM = D_INNER + 2 * G_C * N_C
D_FF = 2816
D_PLE = 256
EPS = 1e-6
PAGE = 128

LANES = 128
SUBLANES = 8
VMEM_LIMIT = 56 * 1024 * 1024

OFF_A = 0
OFF_Q = 2 * C_A
OFF_K = OFF_Q + D_B
OFF_V = OFF_K + D_B
OFF_Z = OFF_V + D_B
OFF_X = OFF_Z + D_INNER
OFF_DT = OFF_X + CONV_C_DIM
W_IN_COLS = OFF_DT + LANES

FF_CHUNK = 256
SSD_CHUNK = 128
CONF_TILE = 512
CONF_ROWS = 32
CONF_PAD = 32
ATT_TILE = 256
HEADS_PER_STEP = LANES // DH_B
LOG2E = 1.4426950408889634


def _const_spec(shape):
    zeros = (0,) * len(shape)
    return pl.BlockSpec(shape, lambda *_: zeros, pipeline_mode=pl.Buffered(1))


def _layer_spec(shape, layer):
    zeros = (0,) * len(shape)
    return pl.BlockSpec((None,) + tuple(shape), lambda *_: (layer,) + zeros, pipeline_mode=pl.Buffered(1))


def _alias_prev(prev, first_in, first_out):
    if prev is None:
        return (), [], {}
    specs = [pl.BlockSpec(memory_space=pl.ANY) for _ in prev]
    return tuple(prev), specs, {first_in + j: first_out + j for j in range(len(prev))}


def _params(*sem):
    return pltpu.CompilerParams(dimension_semantics=sem, vmem_limit_bytes=VMEM_LIMIT)


def _sigmoid(x):
    return 1.0 / (1.0 + jnp.exp(-x))


def _silu(x):
    return x * _sigmoid(x)


def _softplus(x):
    return jnp.maximum(x, 0.0) + jnp.log(1.0 + jnp.exp(-jnp.abs(x)))


def _softplus2(x):
    sign = jnp.uint32(0x80000000)
    neg_abs = lax.bitcast_convert_type(lax.bitcast_convert_type(x, jnp.uint32) | sign, F32)
    return jnp.maximum(x, 0.0) + jnp.log(1.0 + jnp.exp2(neg_abs)) * LOG2E


def _rms(x, g):
    return x * lax.rsqrt(jnp.mean(x * x, axis=-1, keepdims=True) + EPS) * g


def _dot(a, b):
    return jnp.dot(a, b, preferred_element_type=F32)


def _dot_nt(a, b):
    return lax.dot_general(a, b, (((1,), (1,)), ((), ())), preferred_element_type=F32)


def _dot_exact(a, b):
    return jnp.dot(a, b, precision=HIGHEST, preferred_element_type=F32)


def _split_dot(x, m):
    hi = x.astype(BF16)
    lo = (x - hi.astype(F32)).astype(BF16)
    return _dot(hi, m) + _dot(lo, m)


def _in_proj_kernel(h_ref, g_ref, w_ref, gq_ref, gk_ref, seg_ref, *rest):
    a_ref, q_ref, k_ref, v_ref, kb_ref, vb_ref, z_ref, x_ref, dt_ref = rest[-9:]
    u = _rms(h_ref[...], g_ref[...]).astype(BF16)

    def proj(lo, hi):
        return _dot(u, w_ref[:, lo:hi])

    a_ref[...] = proj(OFF_A, OFF_Q)
    z_ref[...] = proj(OFF_Z, OFF_X)
    x_ref[...] = proj(OFF_X, OFF_DT)
    dt_ref[...] = proj(OFF_DT, W_IN_COLS)

    seg = seg_ref[...]

    def head_norm(t, g):
        ms = _split_dot(t * t, seg) * (1.0 / DH_B)
        return t * lax.rsqrt(ms + EPS) * g

    qn = head_norm(proj(OFF_Q, OFF_K), gq_ref[...])
    kn = head_norm(proj(OFF_K, OFF_V), gk_ref[...])
    v = proj(OFF_V, OFF_Z)
    q_ref[...] = (qn * (DH_B ** -0.5 * LOG2E)).astype(BF16)
    k_ref[0] = kn.T
    v_ref[0] = v.T
    kb_ref[...] = kn.astype(BF16)
    vb_ref[...] = v.astype(BF16)


def _in_proj(h, g_mix, w_in, gq, gk, seg, tm, batch, layer, prev_kv):
    t = h.shape[0]
    depth = w_in.shape[0]
    seq = t // batch
    nt = seq // tm
    row = lambda n: pl.BlockSpec((tm, n), lambda i: (i, 0))
    col = pl.BlockSpec((None, 1, D_B, tm), lambda i: (layer, i // nt, 0, i % nt))
    rows = lambda n, d: (row(n), jax.ShapeDtypeStruct((t, n), d))
    cols = (col, jax.ShapeDtypeStruct((depth, batch, D_B, seq), F32))
    outs = (rows(2 * C_A, F32), rows(D_B, BF16), cols, cols, rows(D_B, BF16), rows(D_B, BF16),
            rows(D_INNER, F32), rows(CONV_C_DIM, F32), rows(LANES, F32))
    prev, prev_specs, aliases = _alias_prev(prev_kv, 6, 2)
    return pl.pallas_call(
        _in_proj_kernel,
        grid=(t // tm,),
        in_specs=[row(D_MODEL), _const_spec((1, D_MODEL)), _layer_spec((D_MODEL, W_IN_COLS), layer),
                  _const_spec((1, D_B)), _const_spec((1, D_B)), _const_spec((D_B, D_B))] + prev_specs,
        out_specs=[spec for spec, _ in outs],
        out_shape=[shape for _, shape in outs],
        input_output_aliases=aliases,
        compiler_params=_params("parallel"),
        name="in_proj",
    )(h, g_mix, w_in, gq, gk, seg, *prev)


def _layer_norm_silu(x, g, b):
    mu = jnp.mean(x, axis=-1, keepdims=True)
    xc = x - mu
    y = xc * lax.rsqrt(jnp.mean(xc * xc, axis=-1, keepdims=True) + EPS)
    return _silu(y * g + b)


def _conf_prompt_kernel(a_ref, w_ref, b_ref, g_ref, bb_ref, o_ref, tail_ref, buf, shifted):
    tile = a_ref.shape[0]

    @pl.when(pl.program_id(1) == 0)
    def _():
        buf[0:CONF_PAD, :] = jnp.zeros((CONF_PAD, C_A), F32)

    a = a_ref[...]
    buf[CONF_PAD:CONF_PAD + tile, :] = a[:, :C_A] * _sigmoid(a[:, C_A:])
    n_shift = shifted.shape[1]
    for r in range(1, SUBLANES):
        shifted[r - 1] = buf[r:r + n_shift, :]
    first = CONF_PAD - (CONV_A_W - 1)
    for r0 in range(0, tile, CONF_ROWS):
        acc = jnp.zeros((CONF_ROWS, C_A), F32)
        for k in range(CONV_A_W):
            r = (first + k) % SUBLANES
            base = first + r0 + k - r
            rows = buf[base:base + CONF_ROWS, :] if r == 0 else shifted[r - 1, base:base + CONF_ROWS, :]
            acc = acc + w_ref[k:k + 1, :] * rows
        y = _layer_norm_silu(acc + b_ref[...], g_ref[...], bb_ref[...])
        o_ref[r0:r0 + CONF_ROWS, :] = y.astype(BF16)
    tail = buf[tile:tile + CONF_PAD, :]
    buf[0:CONF_PAD, :] = tail
    tail_ref[0] = tail


def _conf_prompt(a_in, w, b, g, bb, batch):
    t = a_in.shape[0]
    nt = t // batch // CONF_TILE
    return pl.pallas_call(
        _conf_prompt_kernel,
        grid=(batch, nt),
        in_specs=[pl.BlockSpec((CONF_TILE, 2 * C_A), lambda bi, ti: (bi * nt + ti, 0)),
                  _const_spec((CONF_PAD, C_A)), _const_spec((1, C_A)), _const_spec((1, C_A)),
                  _const_spec((1, C_A))],
        out_specs=[pl.BlockSpec((CONF_TILE, C_A), lambda bi, ti: (bi * nt + ti, 0)),
                   pl.BlockSpec((1, CONF_PAD, C_A), lambda bi, ti: (bi, 0, 0))],
        out_shape=[jax.ShapeDtypeStruct((t, C_A), BF16),
                   jax.ShapeDtypeStruct((batch, CONF_PAD, C_A), F32)],
        scratch_shapes=[pltpu.VMEM((CONF_PAD + CONF_TILE, C_A), F32),
                        pltpu.VMEM((SUBLANES - 1, CONF_PAD + CONF_TILE - SUBLANES, C_A), F32)],
        compiler_params=_params("parallel", "arbitrary"),
        name="conformer_prompt",
    )(a_in, w, b, g, bb)


def _attn_prompt_kernel(bias_ref, q_ref, k_ref, v_ref, tri_ref, o_ref, s_ref, cs_ref, acc_ref, carry_ref):
    qi = pl.program_id(1)
    tile = q_ref.shape[0]
    tri = tri_ref[...]
    lane = lax.broadcasted_iota(jnp.int32, (tile, LANES), 1)
    row = lax.broadcasted_iota(jnp.int32, (tile, tile), 0)
    col = lax.broadcasted_iota(jnp.int32, (tile, tile), 1)
    causal = col < row

    heads = []
    for lt in range(D_B // LANES):
        q2 = q_ref[:, lt * LANES:(lt + 1) * LANES].astype(F32)
        for hh in range(HEADS_PER_STEP):
            qh = jnp.where((lane // DH_B) == hh, q2, 0.0).astype(BF16)
            heads.append((lt, qh, bias_ref[lt * HEADS_PER_STEP + hh]))
    n = len(heads)

    def scores(i, start):
        lt, qh, bias = heads[i]
        s_ref[i] = _dot_nt(qh, k_ref[pl.ds(start, tile), lt * LANES:(lt + 1) * LANES]) + bias

    def later_sums(i, masked):
        sp = _softplus2(s_ref[i])
        if masked:
            sp = jnp.where(causal, sp, 0.0)
        cs_ref[i] = _dot(sp.astype(BF16), tri)

    def weigh(i, start, masked):
        lt = heads[i][0]
        cs = cs_ref[i]
        carry = carry_ref[i]
        w = jnp.exp2(s_ref[i] - cs - carry)
        if masked:
            w = jnp.where(causal, w, 0.0)
        acc_ref[i] += _dot(w.astype(BF16), v_ref[pl.ds(start, tile), lt * LANES:(lt + 1) * LANES])
        carry_ref[i] = carry + cs[:, 0:1]

    acc_ref[...] = jnp.zeros_like(acc_ref)
    carry_ref[...] = jnp.zeros_like(carry_ref)
    start0 = pl.multiple_of(qi * tile, tile)
    for i in range(n):
        scores(i, start0)
    for i in range(n):
        later_sums(i, True)
    nxt0 = pl.multiple_of(jnp.maximum(qi - 1, 0) * tile, tile)
    for i in range(n):
        weigh(i, start0, True)
        scores(i, nxt0)
    for i in range(n):
        later_sums(i, False)

    @pl.loop(0, qi)
    def _(j):
        cur = qi - 1 - j
        start = pl.multiple_of(cur * tile, tile)
        nxt = pl.multiple_of(jnp.maximum(cur - 1, 0) * tile, tile)
        for i in range(n):
            weigh(i, start, False)
            scores(i, nxt)
        for i in range(n):
            later_sums(i, False)

    for lt in range(D_B // LANES):
        out = acc_ref[lt * HEADS_PER_STEP]
        for hh in range(1, HEADS_PER_STEP):
            out = jnp.where((lane // DH_B) == hh, acc_ref[lt * HEADS_PER_STEP + hh], out)
        o_ref[:, lt * LANES:(lt + 1) * LANES] = out.astype(BF16)


def _attn_prompt(bias, q, k, v, tri, batch):
    t = q.shape[0]
    seq = t // batch
    nq = seq // ATT_TILE
    resident = lambda: pl.BlockSpec((seq, D_B), lambda b, i, *_: (b, 0), pipeline_mode=pl.Buffered(1))
    grid_spec = pltpu.PrefetchScalarGridSpec(
        num_scalar_prefetch=1,
        grid=(batch, nq),
        in_specs=[pl.BlockSpec((ATT_TILE, D_B), lambda b, i, *_: (b * nq + i, 0)),
                  resident(), resident(),
                  pl.BlockSpec((ATT_TILE, ATT_TILE), lambda b, i, *_: (0, 0))],
        out_specs=pl.BlockSpec((ATT_TILE, D_B), lambda b, i, *_: (b * nq + i, 0)),
        scratch_shapes=[pltpu.VMEM((H_B, ATT_TILE, ATT_TILE), F32),
                        pltpu.VMEM((H_B, ATT_TILE, ATT_TILE), F32),
                        pltpu.VMEM((H_B, ATT_TILE, LANES), F32),
                        pltpu.VMEM((H_B, ATT_TILE, 1), F32)],
    )
    return pl.pallas_call(
        _attn_prompt_kernel,
        grid_spec=grid_spec,
        out_shape=jax.ShapeDtypeStruct((t, D_B), BF16),
        compiler_params=_params("parallel", "arbitrary"),
        name="attn_prompt",
    )(bias, q, k, v, tri)


def _gated_group_norm(y, xs, z, dskip, g):
    y = (y + dskip * xs) * _silu(z)
    gw = D_INNER // G_C
    return jnp.concatenate(
        [_rms(y[:, i * gw:(i + 1) * gw], g[:, i * gw:(i + 1) * gw]) for i in range(G_C)], axis=1)


def _ssd_prompt_kernel(x_ref, z_ref, dt_ref, cw_ref, cb_ref, dtb_ref, alog_ref, dskip_ref, g_ref,
                       tril_ref, e_ref, c_ref, st_ref, cbuf, ybuf):
    q = x_ref.shape[0]

    @pl.when(pl.program_id(1) == 0)
    def _():
        cbuf[0:SUBLANES, :] = jnp.zeros((SUBLANES, CONV_C_DIM), F32)
        st_ref[...] = jnp.zeros_like(st_ref)

    cbuf[SUBLANES:SUBLANES + q, :] = x_ref[...]
    first = SUBLANES - (CONV_C_W - 1)
    conv = cb_ref[...] + sum(cw_ref[k:k + 1, :] * cbuf[first + k:first + k + q, :]
                             for k in range(CONV_C_W))
    cbuf[0:SUBLANES, :] = cbuf[q:q + SUBLANES, :]
    xbc = _silu(conv)
    xs = xbc[:, :D_INNER]
    bm = xbc[:, D_INNER:D_INNER + G_C * N_C].astype(BF16)
    cm = xbc[:, D_INNER + G_C * N_C:].astype(BF16)

    dt = _softplus(dt_ref[...] + dtb_ref[...])
    a_cum = _dot_exact(tril_ref[...], dt * (-jnp.exp(alog_ref[...])))
    e = e_ref[...]
    dt_x = _dot_exact(dt, e)
    a_x = _dot_exact(a_cum, e)
    a_cum_t = a_cum.T
    a_xt = a_x.T
    xdt = xs * dt_x
    exp_a_x = jnp.exp(a_x)

    row = lax.broadcasted_iota(jnp.int32, (q, q), 0)
    col = lax.broadcasted_iota(jnp.int32, (q, q), 1)
    causal = col <= row
    lane = lax.broadcasted_iota(jnp.int32, (q, LANES), 1)
    heads_per_group = H_C // G_C

    for g in range(G_C):
        bg = bm[:, g * N_C:(g + 1) * N_C]
        cg = cm[:, g * N_C:(g + 1) * N_C]
        cb = _dot_nt(cg, bg)
        for pp in range(heads_per_group // HEADS_PER_STEP):
            pair = g * (heads_per_group // HEADS_PER_STEP) + pp
            lo = pair * LANES
            xdt_p = xdt[:, lo:lo + LANES]
            xdt_pb = xdt_p.astype(BF16)
            y_p = None
            for hh in range(HEADS_PER_STEP):
                h = pair * HEADS_PER_STEP + hh
                seg = a_cum[:, h:h + 1] - a_cum_t[h:h + 1, :]
                m = (cb * jnp.exp(jnp.where(causal, seg, -jnp.inf))).astype(BF16)
                r = _dot(m, xdt_pb)
                y_p = r if y_p is None else jnp.where((lane // P_C) == hh, r, y_p)
            h_prev = st_ref[0, lo:lo + LANES, :]
            y_p = y_p + _dot_nt(cg, h_prev.astype(BF16)) * exp_a_x[:, lo:lo + LANES]
            ybuf[:, lo:lo + LANES] = y_p
            a_t = a_xt[lo:lo + LANES, :]
            tot = a_t[:, q - 1:q]
            s_new = _dot((xdt_p.T * jnp.exp(tot - a_t)).astype(BF16), bg)
            st_ref[0, lo:lo + LANES, :] = jnp.exp(tot) * h_prev + s_new

    c_ref[...] = _gated_group_norm(ybuf[...], xs, z_ref[...], dskip_ref[...], g_ref[...]).astype(BF16)


def _ssd_prompt(xbc, z, dt, cw, cb, dtb, alog, dskip, g, tril, e, batch):
    t = xbc.shape[0]
    nc = t // batch // SSD_CHUNK
    row = lambda n: pl.BlockSpec((SSD_CHUNK, n), lambda bi, ci: (bi * nc + ci, 0))
    return pl.pallas_call(
        _ssd_prompt_kernel,
        grid=(batch, nc),
        in_specs=[row(CONV_C_DIM), row(D_INNER), row(LANES),
                  _const_spec((SUBLANES, CONV_C_DIM)), _const_spec((1, CONV_C_DIM)),
                  _const_spec((1, LANES)), _const_spec((1, LANES)), _const_spec((1, D_INNER)),
                  _const_spec((1, D_INNER)), _const_spec((SSD_CHUNK, SSD_CHUNK)),
                  _const_spec((LANES, D_INNER))],
        out_specs=[row(D_INNER), pl.BlockSpec((1, D_INNER, N_C), lambda bi, ci: (bi, 0, 0))],
        out_shape=[jax.ShapeDtypeStruct((t, D_INNER), BF16),
                   jax.ShapeDtypeStruct((batch, D_INNER, N_C), F32)],
        scratch_shapes=[pltpu.VMEM((SUBLANES + SSD_CHUNK, CONV_C_DIM), F32),
                        pltpu.VMEM((SSD_CHUNK, D_INNER), F32)],
        compiler_params=_params("parallel", "arbitrary"),
        name="ssd_prompt",
    )(xbc, z, dt, cw, cb, dtb, alog, dskip, g, tril, e)


def _post_kernel(h_ref, a_ref, b_ref, c_ref, pe_ref, wo_ref, gf_ref, wg_ref, wu_ref, wd_ref,
                 gp_ref, wpg_ref, wp_ref, o_ref):
    h = (h_ref[...] + _dot(a_ref[...], wo_ref[0:C_A, :]) + _dot(b_ref[...], wo_ref[C_A:C_A + D_B, :])
         + _dot(c_ref[...], wo_ref[C_A + D_B:, :]))
    f = _rms(h, gf_ref[...]).astype(BF16)
    acc = jnp.zeros_like(h)
    for c0 in range(0, D_FF, FF_CHUNK):
        gate = _dot(f, wg_ref[:, c0:c0 + FF_CHUNK])
        up = _dot(f, wu_ref[:, c0:c0 + FF_CHUNK])
        acc = acc + _dot((_silu(gate) * up).astype(BF16), wd_ref[c0:c0 + FF_CHUNK, :])
    h = h + acc
    gate = _sigmoid(_dot(_rms(h, gp_ref[...]).astype(BF16), wpg_ref[...]))
    o_ref[...] = h + _dot(pe_ref[...].astype(BF16), wp_ref[...]) * gate


def _post(h, a, b, c, pe, wo, gf, wg, wu, wd, gp, wpg, wp, tm, layer):
    t = h.shape[0]
    row = lambda n: pl.BlockSpec((tm, n), lambda i: (i, 0))
    weight = lambda w: _layer_spec(w.shape[1:], layer)
    return pl.pallas_call(
        _post_kernel,
        grid=(t // tm,),
        in_specs=[row(D_MODEL), row(C_A), row(D_B), row(D_INNER),
                  pl.BlockSpec((None, tm, D_PLE), lambda i: (layer, i, 0)),
                  weight(wo), _const_spec((1, D_MODEL)), weight(wg), weight(wu), weight(wd),
                  _const_spec((1, D_MODEL)), weight(wpg), weight(wp)],
        out_specs=row(D_MODEL),
        out_shape=jax.ShapeDtypeStruct((t, D_MODEL), F32),
        compiler_params=_params("parallel"),
        name="post",
    )(h, a, b, c, pe, wo, gf, wg, wu, wd, gp, wpg, wp)


def _conf_sample_kernel(a_ref, st_ref, w_ref, b_ref, g_ref, bb_ref, o_ref, nst_ref):
    a = a_ref[...]
    glu = a[:, :C_A] * _sigmoid(a[:, C_A:])
    hist = CONV_A_W - 1
    acc = b_ref[...] + w_ref[hist:hist + 1, :] * glu
    for k in range(hist):
        acc = acc + w_ref[k:k + 1, :] * st_ref[k]
    o_ref[...] = _layer_norm_silu(acc, g_ref[...], bb_ref[...]).astype(BF16)
    for k in range(hist - 1):
        nst_ref[k] = st_ref[k + 1]
    nst_ref[hist - 1] = glu


def _conf_sample(a_in, state, w, b, g, bb, nb):
    n = a_in.shape[0]
    hist = CONV_A_W - 1
    row = lambda m: pl.BlockSpec((nb, m), lambda i: (i, 0))
    taps = pl.BlockSpec((hist, nb, C_A), lambda i: (0, i, 0))
    return pl.pallas_call(
        _conf_sample_kernel,
        grid=(n // nb,),
        in_specs=[row(2 * C_A), taps, _const_spec((CONF_PAD, C_A)), _const_spec((1, C_A)),
                  _const_spec((1, C_A)), _const_spec((1, C_A))],
        out_specs=[row(C_A), taps],
        out_shape=[jax.ShapeDtypeStruct((n, C_A), BF16), jax.ShapeDtypeStruct((hist, n, C_A), F32)],
        compiler_params=_params("parallel"),
        name="conformer_sample",
    )(a_in, state, w, b, g, bb)


def _attn_sample_kernel(n_pages, pt_ref, q_ref, bias_ref, tri_ref, later_ref, *refs):
    k_refs = refs[:n_pages]
    v_refs = refs[n_pages:2 * n_pages]
    o_ref = refs[2 * n_pages]
    own = (lax.broadcasted_iota(jnp.int32, (H_B, D_B), 1) // DH_B
           == lax.broadcasted_iota(jnp.int32, (H_B, D_B), 0))
    q_rows = jnp.broadcast_to(q_ref[0].astype(F32), (H_B, D_B))
    qmat = jnp.where(own, q_rows, 0.0).astype(BF16)
    s = jnp.concatenate([_dot(qmat, k_refs[pg][0, 0].astype(BF16)) for pg in range(n_pages)], axis=0)
    s = s + bias_ref[...]
    cs = _split_dot(_softplus2(s), tri_ref[...])
    page_tot = jnp.broadcast_to(cs[:, 0:1], cs.shape)
    cs = cs + _dot_exact(later_ref[...], page_tot)
    w = jnp.exp2(s - cs).astype(BF16)
    acc = jnp.zeros((H_B, D_B), F32)
    for pg in range(n_pages):
        acc = acc + _dot_nt(w[pg * H_B:(pg + 1) * H_B, :], v_refs[pg][0, 0].astype(BF16))
    o_ref[0] = jnp.sum(jnp.where(own, acc, 0.0), axis=0, keepdims=True).astype(BF16)


def _attn_sample(page_table, q, bias, cache_kt, cache_vt, layer):
    n_seq, n_pages = page_table.shape
    rows = n_pages * H_B
    r = jnp.arange(rows)
    later = ((r[:, None] % H_B == r[None, :] % H_B) & (r[None, :] // H_B > r[:, None] // H_B)).astype(F32)
    bias_rows = jnp.broadcast_to(jnp.tile(bias, n_pages)[:, None], (rows, PAGE))
    page_specs = lambda: [pl.BlockSpec((1, 1, D_B, PAGE),
                                       lambda s, pt, pg=pg: (layer, pt[s * n_pages + pg], 0, 0))
                          for pg in range(n_pages)]
    const = lambda shape: pl.BlockSpec(shape, lambda s, pt: (0,) * len(shape))
    grid_spec = pltpu.PrefetchScalarGridSpec(
        num_scalar_prefetch=1,
        grid=(n_seq,),
        in_specs=[pl.BlockSpec((1, 1, D_B), lambda s, pt: (s, 0, 0)),
                  const((rows, PAGE)), const((PAGE, PAGE)), const((rows, rows))]
                 + page_specs() + page_specs(),
        out_specs=pl.BlockSpec((1, 1, D_B), lambda s, pt: (s, 0, 0)),
    )
    out = pl.pallas_call(
        functools.partial(_attn_sample_kernel, n_pages),
        grid_spec=grid_spec,
        out_shape=jax.ShapeDtypeStruct((n_seq, 1, D_B), BF16),
        compiler_params=_params("arbitrary"),
        name="attn_sample",
    )(page_table.reshape(-1), q.reshape(n_seq, 1, D_B), bias_rows, _tri_keys_after(PAGE), later,
      *([cache_kt] * n_pages), *([cache_vt] * n_pages))
    return out.reshape(n_seq, D_B)


def _ssd_sample_prep_kernel(x_ref, st_ref, dt_ref, cw_ref, cb_ref, dtb_ref, alog_ref, e_ref,
                            nst_ref, xs_ref, xdt_t_ref, dec_t_ref, b_ref, c_ref):
    x = x_ref[...]
    hist = CONV_C_W - 1
    conv = cb_ref[...] + cw_ref[hist:hist + 1, :] * x
    for k in range(hist):
        conv = conv + cw_ref[k:k + 1, :] * st_ref[k]
    for k in range(hist - 1):
        nst_ref[k] = st_ref[k + 1]
    nst_ref[hist - 1] = x
    xbc = _silu(conv)
    xs = xbc[:, :D_INNER]
    xs_ref[...] = xs
    b_ref[...] = xbc[:, D_INNER:D_INNER + G_C * N_C]
    c_ref[...] = xbc[:, D_INNER + G_C * N_C:]
    dt = _softplus(dt_ref[...] + dtb_ref[...])
    e = e_ref[...]
    xdt_t_ref[...] = (xs * _dot_exact(dt, e)).T
    dec_t_ref[...] = jnp.exp(_dot_exact(dt * (-jnp.exp(alog_ref[...])), e)).T


def _ssd_sample_prep(xbc, state, dt, cw, cb, dtb, alog, e):
    n = xbc.shape[0]
    shapes = [(state.shape, F32), ((n, D_INNER), F32), ((D_INNER, n), F32), ((D_INNER, n), F32),
              ((n, G_C * N_C), F32), ((n, G_C * N_C), F32)]
    full = lambda s: pl.BlockSpec(s, lambda i: (0,) * len(s))
    return pl.pallas_call(
        _ssd_sample_prep_kernel,
        grid=(1,),
        in_specs=[full(xbc.shape), full(state.shape), full(dt.shape), full(cw.shape), full(cb.shape),
                  full(dtb.shape), full(alog.shape), full(e.shape)],
        out_specs=[full(s) for s, _ in shapes],
        out_shape=[jax.ShapeDtypeStruct(s, d) for s, d in shapes],
        compiler_params=_params("arbitrary"),
        name="ssd_sample_prep",
    )(xbc, state, dt, cw, cb, dtb, alog, e)


def _ssd_sample_state_kernel(xdt_t_ref, dec_t_ref, b_ref, c_ref, h_ref, *rest):
    nh_ref, y_t_ref = rest[-2:]
    n = xdt_t_ref.shape[1]
    per_step = h_ref.shape[0]
    rows = D_INNER // G_C

    @pl.when(pl.program_id(0) == 0)
    def _():
        y_t_ref[...] = jnp.zeros_like(y_t_ref)

    for jj in range(per_step):
        j = pl.program_id(0) * per_step + jj
        mine = lax.broadcasted_iota(jnp.int32, (rows, n), 1) == j
        for g in range(G_C):
            r0 = g * rows
            pick = lambda ref: jnp.sum(jnp.where(mine, ref[r0:r0 + rows, :], 0.0), axis=1, keepdims=True)
            h_new = (pick(dec_t_ref) * h_ref[jj, r0:r0 + rows, :]
                     + pick(xdt_t_ref) * b_ref[jj, :, g * N_C:(g + 1) * N_C])
            nh_ref[jj, r0:r0 + rows, :] = h_new
            y = jnp.sum(h_new * c_ref[jj, :, g * N_C:(g + 1) * N_C], axis=1, keepdims=True)
            y_t_ref[r0:r0 + rows, :] = jnp.where(mine, y, y_t_ref[r0:r0 + rows, :])


def _ssd_sample_state(xdt_t, dec_t, bm, cm, state, layer, prev_state, per_step):
    depth, n = state.shape[:2]
    full = lambda s: pl.BlockSpec(s, lambda j: (0,) * len(s))
    per_seq = pl.BlockSpec((per_step, 1, G_C * N_C), lambda j: (j, 0, 0))
    slab = pl.BlockSpec((None, per_step, D_INNER, N_C), lambda j: (layer, j, 0, 0))
    prev, prev_specs, aliases = _alias_prev(prev_state, 5, 0)
    return pl.pallas_call(
        _ssd_sample_state_kernel,
        grid=(n // per_step,),
        in_specs=[full(xdt_t.shape), full(dec_t.shape), per_seq, per_seq, slab] + prev_specs,
        out_specs=[slab, full((D_INNER, n))],
        out_shape=[jax.ShapeDtypeStruct(state.shape, F32), jax.ShapeDtypeStruct((D_INNER, n), F32)],
        input_output_aliases=aliases,
        compiler_params=_params("arbitrary"),
        name="ssd_sample_state",
    )(xdt_t, dec_t, bm.reshape(n, 1, -1), cm.reshape(n, 1, -1), state, *prev)


def _ssd_sample_out_kernel(y_t_ref, xs_ref, z_ref, dskip_ref, g_ref, c_ref):
    c_ref[...] = _gated_group_norm(y_t_ref[...].T, xs_ref[...], z_ref[...], dskip_ref[...],
                                   g_ref[...]).astype(BF16)


def _ssd_sample_out(y_t, xs, z, dskip, g):
    n = xs.shape[0]
    full = lambda s: pl.BlockSpec(s, lambda i: (0,) * len(s))
    return pl.pallas_call(
        _ssd_sample_out_kernel,
        grid=(1,),
        in_specs=[full(y_t.shape), full(xs.shape), full(z.shape), full(dskip.shape), full(g.shape)],
        out_specs=full((n, D_INNER)),
        out_shape=jax.ShapeDtypeStruct((n, D_INNER), BF16),
        compiler_params=_params("arbitrary"),
        name="ssd_sample_out",
    )(y_t, xs, z, dskip, g)


def _token_tile(t):
    for tile in (512, 256):
        if t % tile == 0:
            return tile
    return t


def _tri_keys_after(n):
    i = jnp.arange(n)
    return (i[:, None] >= i[None, :]).astype(BF16)


def kernel(x_prompt, x_sample, cache_k, cache_v, state_conv_a, state_conv_ssm, state_ssm, page_table,
           p_prompt, p_sample, g_mix, w_in, conv_a_w, conv_a_b, ln_a_g, ln_a_b, g_q, g_k, sb_bias,
           conv_ssm_w, conv_ssm_b, dt_bias, a_log, d_skip, g_ssm, w_out, g_ffn, w_ffn_gate, w_ffn_up,
           w_ffn_down, g_ple, w_ple_gate, w_ple):
    depth = w_in.shape[0]
    batch, seq, _ = x_prompt.shape
    n_seq = x_sample.shape[0]
    tp = batch * seq
    assert x_sample.shape[1] == 1
    assert seq % CONF_TILE == 0 and seq % ATT_TILE == 0 and seq % SSD_CHUNK == 0

    pad_lanes = lambda v: jnp.pad(v, ((0, 0), (0, LANES - v.shape[1])))
    seg = (jnp.arange(D_B)[:, None] // DH_B == jnp.arange(D_B)[None, :] // DH_B).astype(BF16)
    head_lanes = (jnp.arange(LANES)[:, None] == jnp.arange(D_INNER)[None, :] // P_C).astype(F32)
    tril = (jnp.arange(SSD_CHUNK)[:, None] >= jnp.arange(SSD_CHUNK)[None, :]).astype(F32)
    tri_att = _tri_keys_after(ATT_TILE)

    n_pool = cache_k.shape[1]
    cache_kt = jnp.transpose(cache_k, (0, 1, 3, 4, 2)).reshape(depth, n_pool, D_B, PAGE)
    cache_vt = jnp.transpose(cache_v, (0, 1, 3, 4, 2)).reshape(depth, n_pool, D_B, PAGE)
    conv_a_taps = jnp.transpose(state_conv_a, (0, 2, 1, 3))
    conv_c_taps = jnp.transpose(state_conv_ssm, (0, 2, 1, 3))
    ssm_flat = state_ssm.reshape(depth, n_seq, D_INNER, N_C)

    h_p = x_prompt.reshape(tp, D_MODEL)
    h_s = x_sample.reshape(n_seq, D_MODEL)
    pe_p = p_prompt.reshape(depth, tp, D_PLE)
    pe_s = p_sample.reshape(depth, n_seq, D_PLE)
    w_in_b = jnp.pad(w_in, ((0, 0), (0, 0), (0, W_IN_COLS - w_in.shape[2]))).astype(BF16)
    wo, wg, wu, wd, wpg, wp = (w.astype(BF16) for w in (w_out, w_ffn_gate, w_ffn_up, w_ffn_down,
                                                        w_ple_gate, w_ple))
    tile_p, tile_s = _token_tile(seq), _token_tile(n_seq)
    seqs_per_step = 4 if n_seq % 4 == 0 else 1
    kv_p = kv_s = ssm_s = None
    outs = {name: [] for name in ("cap", "ccp", "ssp", "cas", "ccs")}
    for i in range(depth):
        g_mix_i = g_mix[i][None]
        gq = jnp.tile(g_q[i], H_B)[None]
        gk = jnp.tile(g_k[i], H_B)[None]
        conv_a_w_i = jnp.pad(conv_a_w[i], ((0, CONF_PAD - CONV_A_W), (0, 0)))
        conv_c_w_i = jnp.pad(conv_ssm_w[i], ((0, SUBLANES - CONV_C_W), (0, 0)))
        dtb = pad_lanes(dt_bias[i][None])
        alog = pad_lanes(a_log[i][None])
        dskip = jnp.repeat(d_skip[i], P_C)[None]
        bias2 = sb_bias[i] * LOG2E
        post_w = (wo, g_ffn[i][None], wg, wu, wd, g_ple[i][None], wpg, wp)

        a_in, q, kt, vt, kb, vb, z, xbc, dt = _in_proj(h_p, g_mix_i, w_in_b, gq, gk, seg, tile_p,
                                                       batch, i, kv_p)
        kv_p = (kt, vt)
        a_out, a_tail = _conf_prompt(a_in, conv_a_w_i, conv_a_b[i][None], ln_a_g[i][None],
                                     ln_a_b[i][None], batch)
        b_out = _attn_prompt(bias2, q, kb, vb, tri_att, batch)
        c_out, ssm_p = _ssd_prompt(xbc, z, dt, conv_c_w_i, conv_ssm_b[i][None], dtb, alog, dskip,
                                   g_ssm[i][None], tril, head_lanes, batch)
        h_p = _post(h_p, a_out, b_out, c_out, pe_p, *post_w, tile_p, i)
        outs["cap"].append(a_tail[:, CONF_PAD - (CONV_A_W - 1):])
        outs["ccp"].append(xbc.reshape(batch, seq, CONV_C_DIM)[:, seq - (CONV_C_W - 1):])
        outs["ssp"].append(ssm_p.reshape(batch, H_C, P_C, N_C))

        a_in, q, kt, vt, _, _, z, xbc, dt = _in_proj(h_s, g_mix_i, w_in_b, gq, gk, seg, tile_s,
                                                     1, i, kv_s)
        kv_s = (kt, vt)
        a_out, conv_a_new = _conf_sample(a_in, conv_a_taps[i], conv_a_w_i, conv_a_b[i][None],
                                         ln_a_g[i][None], ln_a_b[i][None], min(n_seq, 32))
        b_out = _attn_sample(page_table, q, bias2, cache_kt, cache_vt, i)
        conv_c_new, xs, xdt_t, dec_t, bm, cm = _ssd_sample_prep(
            xbc, conv_c_taps[i], dt, conv_c_w_i, conv_ssm_b[i][None], dtb, alog, head_lanes)
        ssm_s, y_t = _ssd_sample_state(xdt_t, dec_t, bm, cm, ssm_flat, i,
                                       None if ssm_s is None else (ssm_s,), seqs_per_step)
        c_out = _ssd_sample_out(y_t, xs, z, dskip, g_ssm[i][None])
        h_s = _post(h_s, a_out, b_out, c_out, pe_s, *post_w, tile_s, i)
        outs["cas"].append(jnp.transpose(conv_a_new, (1, 0, 2)))
        outs["ccs"].append(jnp.transpose(conv_c_new, (1, 0, 2)))

    st = lambda name: jnp.stack(outs[name])
    heads_last = lambda x: jnp.transpose(x.reshape(depth, x.shape[1], H_B, DH_B, x.shape[3]), (0, 1, 4, 2, 3))
    k_p, v_p = (heads_last(x) for x in kv_p)
    k_s, v_s = (jnp.transpose(heads_last(x), (0, 2, 1, 3, 4)) for x in kv_s)
    return (h_p.reshape(batch, seq, D_MODEL), h_s.reshape(n_seq, 1, D_MODEL),
            k_p, v_p, st("cap"), st("ccp"), st("ssp"),
            k_s, v_s, st("cas"), st("ccs"), ssm_s.reshape(depth, n_seq, H_C, P_C, N_C))
```

```python
import functools
import math

import jax
import jax.numpy as jnp
from jax import lax
from jax.experimental import pallas as pl
from jax.experimental.pallas import tpu as pltpu

F32 = jnp.float32
BF16 = jnp.bfloat16

D_MODEL = 1024
C_A = 512
CONV_A_W = 31
H_B = 8
DH_B = 64
D_B = H_B * DH_B
D_INNER = 1024
H_C = 16
P_C = 64
G_C = 2
N_C = 128
CONV_C_W = 4
CONV_C_DIM = D_INNER + 2 * G_C * N_C
D_FF = 2816
D_PLE = 256
EPS = 1e-6
PAGE = 128

LANES = 128
SUBLANES = 8
VMEM_LIMIT = 56 * 1024 * 1024

OFF_A = 0
OFF_Q = 2 * C_A
OFF_K = OFF_Q + D_B
OFF_V = OFF_K + D_B
OFF_Z = OFF_V + D_B
OFF_X = OFF_Z + D_INNER
OFF_DT = OFF_X + CONV_C_DIM
W_IN_COLS = OFF_DT + LANES

FF_CHUNK = 256
SSD_CHUNK = 128
CONF_TILE = 512
CONF_ROWS = 32
CONF_PAD = 32
ATT_TILE = 256
HEADS_PER_STEP = LANES // DH_B
LOG2E = 1.4426950408889634


def _const_spec(shape):
    zeros = (0,) * len(shape)
    return pl.BlockSpec(shape, lambda *_: zeros, pipeline_mode=pl.Buffered(1))


def _layer_spec(shape, layer):
    zeros = (0,) * len(shape)
    return pl.BlockSpec((None,) + tuple(shape), lambda *_: (layer,) + zeros, pipeline_mode=pl.Buffered(1))


def _alias_prev(prev, first_in, first_out):
    if prev is None:
        return (), [], {}
    specs = [pl.BlockSpec(memory_space=pl.ANY) for _ in prev]
    return tuple(prev), specs, {first_in + j: first_out + j for j in range(len(prev))}


def _params(*sem):
    return pltpu.CompilerParams(dimension_semantics=sem, vmem_limit_bytes=VMEM_LIMIT)


def _sigmoid(x):
    return 0.5 * jnp.tanh(0.5 * x) + 0.5


def _silu(x):
    return x * _sigmoid(x)


def _softplus(x):
    return jnp.maximum(x, 0.0) + jnp.log(1.0 + jnp.exp(-jnp.abs(x)))


def _softplus2(x):
    sign = jnp.uint32(0x80000000)
    neg_abs = lax.bitcast_convert_type(lax.bitcast_convert_type(x, jnp.uint32) | sign, F32)
    return jnp.maximum(x, 0.0) + jnp.log(1.0 + jnp.exp2(neg_abs)) * LOG2E


def _rms(x, g):
    return x * lax.rsqrt(jnp.mean(x * x, axis=-1, keepdims=True) + EPS) * g


def _dot(a, b):
    return jnp.dot(a, b, preferred_element_type=F32)


def _dot_nt(a, b):
    return lax.dot_general(a, b, (((1,), (1,)), ((), ())), preferred_element_type=F32)


def _three_parts(x):
    hi = x.astype(BF16)
    rest = x - hi.astype(F32)
    mid = rest.astype(BF16)
    return hi, mid, (rest - mid.astype(F32)).astype(BF16)


def _spread(x, m):
    hi, mid, lo = _three_parts(x)
    return _dot(hi, m) + _dot(mid, m) + _dot(lo, m)


def _collect(m, x):
    hi, mid, lo = _three_parts(x)
    return _dot(m, hi) + _dot(m, mid) + _dot(m, lo)


def _split_dot(x, m):
    hi = x.astype(BF16)
    lo = (x - hi.astype(F32)).astype(BF16)
    return _dot(hi, m) + _dot(lo, m)


def _in_proj_kernel(h_ref, g_ref, w_ref, gq_ref, gk_ref, seg_ref, *rest):
    a_ref, q_ref, k_ref, v_ref, kb_ref, vb_ref, z_ref, x_ref, dt_ref = rest[-9:]
    u = _rms(h_ref[...], g_ref[...]).astype(BF16)

    def proj(lo, hi):
        return _dot(u, w_ref[:, lo:hi])

    a_ref[...] = proj(OFF_A, OFF_Q)
    z_ref[...] = proj(OFF_Z, OFF_X)
    x_ref[...] = proj(OFF_X, OFF_DT)
    dt_ref[...] = proj(OFF_DT, W_IN_COLS)

    seg = seg_ref[...]

    def head_norm(t, g):
        ms = _split_dot(t * t, seg) * (1.0 / DH_B)
        return t * lax.rsqrt(ms + EPS) * g

    qn = head_norm(proj(OFF_Q, OFF_K), gq_ref[...])
    kn = head_norm(proj(OFF_K, OFF_V), gk_ref[...])
    v = proj(OFF_V, OFF_Z)
    q_ref[...] = (qn * (DH_B ** -0.5 * LOG2E)).astype(BF16)
    k_ref[0] = kn.T
    vt = v.T
    v_ref[0] = vt
    kb_ref[...] = kn.astype(BF16)
    vb_ref[0] = vt.astype(BF16)


def _in_proj(h, g_mix, w_in, gq, gk, seg, tm, batch, layer, prev_kv):
    t = h.shape[0]
    depth = w_in.shape[0]
    seq = t // batch
    nt = seq // tm
    row = lambda n: pl.BlockSpec((tm, n), lambda i: (i, 0))
    col = pl.BlockSpec((None, 1, D_B, tm), lambda i: (layer, i // nt, 0, i % nt))
    rows = lambda n, d: (row(n), jax.ShapeDtypeStruct((t, n), d))
    cols = (col, jax.ShapeDtypeStruct((depth, batch, D_B, seq), F32))
    cols_bf = (pl.BlockSpec((1, D_B, tm), lambda i: (i // nt, 0, i % nt)),
               jax.ShapeDtypeStruct((batch, D_B, seq), BF16))
    outs = (rows(2 * C_A, F32), rows(D_B, BF16), cols, cols, rows(D_B, BF16), cols_bf,
            rows(D_INNER, F32), rows(CONV_C_DIM, F32), rows(LANES, F32))
    prev, prev_specs, aliases = _alias_prev(prev_kv, 6, 2)
    return pl.pallas_call(
        _in_proj_kernel,
        grid=(t // tm,),
        in_specs=[row(D_MODEL), _const_spec((1, D_MODEL)), _layer_spec((D_MODEL, W_IN_COLS), layer),
                  _const_spec((1, D_B)), _const_spec((1, D_B)), _const_spec((D_B, D_B))] + prev_specs,
        out_specs=[spec for spec, _ in outs],
        out_shape=[shape for _, shape in outs],
        input_output_aliases=aliases,
        compiler_params=_params("parallel"),
        name="in_proj",
    )(h, g_mix, w_in, gq, gk, seg, *prev)


def _layer_norm_silu(x, g, b):
    mu = jnp.mean(x, axis=-1, keepdims=True)
    xc = x - mu
    y = xc * lax.rsqrt(jnp.mean(xc * xc, axis=-1, keepdims=True) + EPS)
    return _silu(y * g + b)


def _conf_prompt_kernel(a_ref, w_ref, b_ref, g_ref, bb_ref, o_ref, tail_ref, buf, shifted):
    tile = a_ref.shape[0]

    @pl.when(pl.program_id(1) == 0)
    def _():
        buf[0:CONF_PAD, :] = jnp.zeros((CONF_PAD, C_A), F32)

    a = a_ref[...]
    buf[CONF_PAD:CONF_PAD + tile, :] = a[:, :C_A] * _sigmoid(a[:, C_A:])
    n_shift = shifted.shape[1]
    for r in range(1, SUBLANES):
        shifted[r - 1] = buf[r:r + n_shift, :]
    first = CONF_PAD - (CONV_A_W - 1)
    for r0 in range(0, tile, CONF_ROWS):
        acc = jnp.zeros((CONF_ROWS, C_A), F32)
        for k in range(CONV_A_W):
            r = (first + k) % SUBLANES
            base = first + r0 + k - r
            rows = buf[base:base + CONF_ROWS, :] if r == 0 else shifted[r - 1, base:base + CONF_ROWS, :]
            acc = acc + w_ref[k:k + 1, :] * rows
        y = _layer_norm_silu(acc + b_ref[...], g_ref[...], bb_ref[...])
        o_ref[r0:r0 + CONF_ROWS, :] = y.astype(BF16)
    tail = buf[tile:tile + CONF_PAD, :]
    buf[0:CONF_PAD, :] = tail
    tail_ref[0] = tail


def _conf_prompt(a_in, w, b, g, bb, batch):
    t = a_in.shape[0]
    nt = t // batch // CONF_TILE
    return pl.pallas_call(
        _conf_prompt_kernel,
        grid=(batch, nt),
        in_specs=[pl.BlockSpec((CONF_TILE, 2 * C_A), lambda bi, ti: (bi * nt + ti, 0)),
                  _const_spec((CONF_PAD, C_A)), _const_spec((1, C_A)), _const_spec((1, C_A)),
                  _const_spec((1, C_A))],
        out_specs=[pl.BlockSpec((CONF_TILE, C_A), lambda bi, ti: (bi * nt + ti, 0)),
                   pl.BlockSpec((1, CONF_PAD, C_A), lambda bi, ti: (bi, 0, 0))],
        out_shape=[jax.ShapeDtypeStruct((t, C_A), BF16),
                   jax.ShapeDtypeStruct((batch, CONF_PAD, C_A), F32)],
        scratch_shapes=[pltpu.VMEM((CONF_PAD + CONF_TILE, C_A), F32),
                        pltpu.VMEM((SUBLANES - 1, CONF_PAD + CONF_TILE - SUBLANES, C_A), F32)],
        compiler_params=_params("parallel", "arbitrary"),
        name="conformer_prompt",
    )(a_in, w, b, g, bb)


def _attn_prompt_kernel(bias_ref, q_ref, k_ref, vt_ref, tri_ref, o_ref, s_ref, cs_ref, acc_ref, carry_ref):
    qi = pl.program_id(1)
    tile = q_ref.shape[0]
    tri = tri_ref[...]
    lane = lax.broadcasted_iota(jnp.int32, (tile, LANES), 1)
    key = lax.broadcasted_iota(jnp.int32, (tile, tile), 0)
    query = lax.broadcasted_iota(jnp.int32, (tile, tile), 1)
    causal = key < query

    heads = []
    for lt in range(D_B // LANES):
        q2 = q_ref[:, lt * LANES:(lt + 1) * LANES].astype(F32)
        for hh in range(HEADS_PER_STEP):
            qh = jnp.where((lane // DH_B) == hh, q2, 0.0).astype(BF16)
            heads.append((lt, qh, bias_ref[lt * HEADS_PER_STEP + hh]))
    n = len(heads)

    def scores(i, start):
        lt, qh, bias = heads[i]
        s_ref[i] = _dot_nt(k_ref[pl.ds(start, tile), lt * LANES:(lt + 1) * LANES], qh) + bias

    def later_sums(i, masked):
        sp = _softplus2(s_ref[i])
        if masked:
            sp = jnp.where(causal, sp, 0.0)
        cs_ref[i] = _dot(tri, sp.astype(BF16))

    def weigh(i, start, masked):
        lt = heads[i][0]
        cs = cs_ref[i]
        carry = carry_ref[i]
        x = s_ref[i] - cs - carry
        if masked:
            x = jnp.where(causal, x, -jnp.inf)
        acc_ref[i] += _dot(vt_ref[lt * LANES:(lt + 1) * LANES, pl.ds(start, tile)], jnp.exp2(x).astype(BF16))
        carry_ref[i] = carry + cs[0:1, :]

    acc_ref[...] = jnp.zeros_like(acc_ref)
    carry_ref[...] = jnp.zeros_like(carry_ref)
    start0 = pl.multiple_of(qi * tile, tile)
    for i in range(n):
        scores(i, start0)
    for i in range(n):
        later_sums(i, True)
    nxt0 = pl.multiple_of(jnp.maximum(qi - 1, 0) * tile, tile)
    for i in range(n):
        weigh(i, start0, True)
        scores(i, nxt0)
    for i in range(n):
        later_sums(i, False)

    @pl.loop(0, qi)
    def _(j):
        cur = qi - 1 - j
        start = pl.multiple_of(cur * tile, tile)
        nxt = pl.multiple_of(jnp.maximum(cur - 1, 0) * tile, tile)
        for i in range(n):
            weigh(i, start, False)
            scores(i, nxt)
        for i in range(n):
            later_sums(i, False)

    row = lax.broadcasted_iota(jnp.int32, (LANES, tile), 0)
    for lt in range(D_B // LANES):
        out = acc_ref[lt * HEADS_PER_STEP]
        for hh in range(1, HEADS_PER_STEP):
            out = jnp.where((row // DH_B) == hh, acc_ref[lt * HEADS_PER_STEP + hh], out)
        o_ref[:, lt * LANES:(lt + 1) * LANES] = out.T.astype(BF16)


def _attn_prompt(bias, q, k, vt, tri, batch):
    t = q.shape[0]
    seq = t // batch
    nq = seq // ATT_TILE
    grid_spec = pltpu.PrefetchScalarGridSpec(
        num_scalar_prefetch=1,
        grid=(batch, nq),
        in_specs=[pl.BlockSpec((ATT_TILE, D_B), lambda b, i, *_: (b * nq + i, 0)),
                  pl.BlockSpec((seq, D_B), lambda b, i, *_: (b, 0), pipeline_mode=pl.Buffered(1)),
                  pl.BlockSpec((None, D_B, seq), lambda b, i, *_: (b, 0, 0), pipeline_mode=pl.Buffered(1)),
                  pl.BlockSpec((ATT_TILE, ATT_TILE), lambda b, i, *_: (0, 0))],
        out_specs=pl.BlockSpec((ATT_TILE, D_B), lambda b, i, *_: (b * nq + i, 0)),
        scratch_shapes=[pltpu.VMEM((H_B, ATT_TILE, ATT_TILE), F32),
                        pltpu.VMEM((H_B, ATT_TILE, ATT_TILE), F32),
                        pltpu.VMEM((H_B, LANES, ATT_TILE), F32),
                        pltpu.VMEM((H_B, 1, ATT_TILE), F32)],
    )
    return pl.pallas_call(
        _attn_prompt_kernel,
        grid_spec=grid_spec,
        out_shape=jax.ShapeDtypeStruct((t, D_B), BF16),
        compiler_params=_params("parallel", "arbitrary"),
        name="attn_prompt",
    )(bias, q, k, vt, tri)


def _gated_group_norm(y, xs, z, dskip, g):
    y = (y + dskip * xs) * _silu(z)
    gw = D_INNER // G_C
    return jnp.concatenate(
        [_rms(y[:, i * gw:(i + 1) * gw], g[:, i * gw:(i + 1) * gw]) for i in range(G_C)], axis=1)


def _ssd_prompt_kernel(x_ref, z_ref, dt_ref, cw_ref, cb_ref, dtb_ref, alog_ref, dskip_ref, g_ref,
                       tril_ref, e_ref, c_ref, st_ref, cbuf, ybuf):
    q = x_ref.shape[0]

    @pl.when(pl.program_id(1) == 0)
    def _():
        cbuf[0:SUBLANES, :] = jnp.zeros((SUBLANES, CONV_C_DIM), F32)
        st_ref[...] = jnp.zeros_like(st_ref)

    cbuf[SUBLANES:SUBLANES + q, :] = x_ref[...]
    first = SUBLANES - (CONV_C_W - 1)
    conv = cb_ref[...] + sum(cw_ref[k:k + 1, :] * cbuf[first + k:first + k + q, :]
                             for k in range(CONV_C_W))
    cbuf[0:SUBLANES, :] = cbuf[q:q + SUBLANES, :]
    xbc = _silu(conv)
    xs = xbc[:, :D_INNER]
    bm = xbc[:, D_INNER:D_INNER + G_C * N_C].astype(BF16)
    cm = xbc[:, D_INNER + G_C * N_C:].astype(BF16)

    dt = _softplus(dt_ref[...] + dtb_ref[...])
    a_cum = _collect(tril_ref[...], dt * (-jnp.exp(alog_ref[...])))
    e = e_ref[...]
    dt_x = _spread(dt, e)
    a_x = _spread(a_cum, e)
    a_cum_t = a_cum.T
    a_xt = a_x.T
    xdt = xs * dt_x
    exp_a_x = jnp.exp(a_x)

    row = lax.broadcasted_iota(jnp.int32, (q, q), 0)
    col = lax.broadcasted_iota(jnp.int32, (q, q), 1)
    causal = col <= row
    lane = lax.broadcasted_iota(jnp.int32, (q, LANES), 1)
    heads_per_group = H_C // G_C

    for g in range(G_C):
        bg = bm[:, g * N_C:(g + 1) * N_C]
        cg = cm[:, g * N_C:(g + 1) * N_C]
        cb = _dot_nt(cg, bg)
        for pp in range(heads_per_group // HEADS_PER_STEP):
            pair = g * (heads_per_group // HEADS_PER_STEP) + pp
            lo = pair * LANES
            xdt_p = xdt[:, lo:lo + LANES]
            xdt_pb = xdt_p.astype(BF16)
            y_p = None
            for hh in range(HEADS_PER_STEP):
                h = pair * HEADS_PER_STEP + hh
                seg = a_cum[:, h:h + 1] - a_cum_t[h:h + 1, :]
                m = (cb * jnp.exp(jnp.where(causal, seg, -jnp.inf))).astype(BF16)
                r = _dot(m, xdt_pb)
                y_p = r if y_p is None else jnp.where((lane // P_C) == hh, r, y_p)
            h_prev = st_ref[0, lo:lo + LANES, :]
            y_p = y_p + _dot_nt(cg, h_prev.astype(BF16)) * exp_a_x[:, lo:lo + LANES]
            ybuf[:, lo:lo + LANES] = y_p
            a_t = a_xt[lo:lo + LANES, :]
            tot = a_t[:, q - 1:q]
            s_new = _dot((xdt_p.T * jnp.exp(tot - a_t)).astype(BF16), bg)
            st_ref[0, lo:lo + LANES, :] = jnp.exp(tot) * h_prev + s_new

    c_ref[...] = _gated_group_norm(ybuf[...], xs, z_ref[...], dskip_ref[...], g_ref[...]).astype(BF16)


def _ssd_prompt(xbc, z, dt, cw, cb, dtb, alog, dskip, g, tril, e, batch):
    t = xbc.shape[0]
    nc = t // batch // SSD_CHUNK
    row = lambda n: pl.BlockSpec((SSD_CHUNK, n), lambda bi, ci: (bi * nc + ci, 0))
    return pl.pallas_call(
        _ssd_prompt_kernel,
        grid=(batch, nc),
        in_specs=[row(CONV_C_DIM), row(D_INNER), row(LANES),
                  _const_spec((SUBLANES, CONV_C_DIM)), _const_spec((1, CONV_C_DIM)),
                  _const_spec((1, LANES)), _const_spec((1, LANES)), _const_spec((1, D_INNER)),
                  _const_spec((1, D_INNER)), _const_spec((SSD_CHUNK, SSD_CHUNK)),
                  _const_spec((LANES, D_INNER))],
        out_specs=[row(D_INNER), pl.BlockSpec((1, D_INNER, N_C), lambda bi, ci: (bi, 0, 0))],
        out_shape=[jax.ShapeDtypeStruct((t, D_INNER), BF16),
                   jax.ShapeDtypeStruct((batch, D_INNER, N_C), F32)],
        scratch_shapes=[pltpu.VMEM((SUBLANES + SSD_CHUNK, CONV_C_DIM), F32),
                        pltpu.VMEM((SSD_CHUNK, D_INNER), F32)],
        compiler_params=_params("parallel", "arbitrary"),
        name="ssd_prompt",
    )(xbc, z, dt, cw, cb, dtb, alog, dskip, g, tril, e)


def _post_kernel(h_ref, a_ref, b_ref, c_ref, pe_ref, wo_ref, gf_ref, wg_ref, wu_ref, wd_ref,
                 gp_ref, wpg_ref, wp_ref, o_ref):
    h = (h_ref[...] + _dot(a_ref[...], wo_ref[0:C_A, :]) + _dot(b_ref[...], wo_ref[C_A:C_A + D_B, :])
         + _dot(c_ref[...], wo_ref[C_A + D_B:, :]))
    f = _rms(h, gf_ref[...]).astype(BF16)
    acc = jnp.zeros_like(h)
    for c0 in range(0, D_FF, FF_CHUNK):
        gate = _dot(f, wg_ref[:, c0:c0 + FF_CHUNK])
        up = _dot(f, wu_ref[:, c0:c0 + FF_CHUNK])
        acc = acc + _dot((_silu(gate) * up).astype(BF16), wd_ref[c0:c0 + FF_CHUNK, :])
    h = h + acc
    gate = _sigmoid(_dot(_rms(h, gp_ref[...]).astype(BF16), wpg_ref[...]))
    o_ref[...] = h + _dot(pe_ref[...].astype(BF16), wp_ref[...]) * gate


def _post(h, a, b, c, pe, wo, gf, wg, wu, wd, gp, wpg, wp, tm, layer):
    t = h.shape[0]
    row = lambda n: pl.BlockSpec((tm, n), lambda i: (i, 0))
    weight = lambda w: _layer_spec(w.shape[1:], layer)
    return pl.pallas_call(
        _post_kernel,
        grid=(t // tm,),
        in_specs=[row(D_MODEL), row(C_A), row(D_B), row(D_INNER),
                  pl.BlockSpec((None, tm, D_PLE), lambda i: (layer, i, 0)),
                  weight(wo), _const_spec((1, D_MODEL)), weight(wg), weight(wu), weight(wd),
                  _const_spec((1, D_MODEL)), weight(wpg), weight(wp)],
        out_specs=row(D_MODEL),
        out_shape=jax.ShapeDtypeStruct((t, D_MODEL), F32),
        compiler_params=_params("parallel"),
        name="post",
    )(h, a, b, c, pe, wo, gf, wg, wu, wd, gp, wpg, wp)


def _conf_sample_kernel(a_ref, st_ref, w_ref, b_ref, g_ref, bb_ref, o_ref, nst_ref):
    a = a_ref[...]
    glu = a[:, :C_A] * _sigmoid(a[:, C_A:])
    hist = CONV_A_W - 1
    acc = b_ref[...] + w_ref[hist:hist + 1, :] * glu
    for k in range(hist):
        acc = acc + w_ref[k:k + 1, :] * st_ref[k]
    o_ref[...] = _layer_norm_silu(acc, g_ref[...], bb_ref[...]).astype(BF16)
    for k in range(hist - 1):
        nst_ref[k] = st_ref[k + 1]
    nst_ref[hist - 1] = glu


def _conf_sample(a_in, state, w, b, g, bb, nb):
    n = a_in.shape[0]
    hist = CONV_A_W - 1
    row = lambda m: pl.BlockSpec((nb, m), lambda i: (i, 0))
    taps = pl.BlockSpec((hist, nb, C_A), lambda i: (0, i, 0))
    return pl.pallas_call(
        _conf_sample_kernel,
        grid=(n // nb,),
        in_specs=[row(2 * C_A), taps, _const_spec((CONF_PAD, C_A)), _const_spec((1, C_A)),
                  _const_spec((1, C_A)), _const_spec((1, C_A))],
        out_specs=[row(C_A), taps],
        out_shape=[jax.ShapeDtypeStruct((n, C_A), BF16), jax.ShapeDtypeStruct((hist, n, C_A), F32)],
        compiler_params=_params("parallel"),
        name="conformer_sample",
    )(a_in, state, w, b, g, bb)


def _attn_sample_kernel(n_pages, pt_ref, q_ref, bias_ref, tri_ref, later_ref, *refs):
    k_refs = refs[:n_pages]
    v_refs = refs[n_pages:2 * n_pages]
    o_ref = refs[2 * n_pages]
    own = (lax.broadcasted_iota(jnp.int32, (H_B, D_B), 1) // DH_B
           == lax.broadcasted_iota(jnp.int32, (H_B, D_B), 0))
    q_rows = jnp.broadcast_to(q_ref[0].astype(F32), (H_B, D_B))
    qmat = jnp.where(own, q_rows, 0.0).astype(BF16)
    s = jnp.concatenate([_dot(qmat, k_refs[pg][0, 0].astype(BF16)) for pg in range(n_pages)], axis=0)
    s = s + bias_ref[...]
    cs = _split_dot(_softplus2(s), tri_ref[...])
    page_tot = jnp.broadcast_to(cs[:, 0:1], cs.shape)
    cs = cs + _collect(later_ref[...], page_tot)
    w = jnp.exp2(s - cs).astype(BF16)
    acc = jnp.zeros((H_B, D_B), F32)
    for pg in range(n_pages):
        acc = acc + _dot_nt(w[pg * H_B:(pg + 1) * H_B, :], v_refs[pg][0, 0].astype(BF16))
    o_ref[0] = jnp.sum(jnp.where(own, acc, 0.0), axis=0, keepdims=True).astype(BF16)


def _attn_sample(page_table, q, bias, cache_kt, cache_vt, layer):
    n_seq, n_pages = page_table.shape
    rows = n_pages * H_B
    r = jnp.arange(rows)
    later = ((r[:, None] % H_B == r[None, :] % H_B) & (r[None, :] // H_B > r[:, None] // H_B)).astype(BF16)
    bias_rows = jnp.broadcast_to(jnp.tile(bias, n_pages)[:, None], (rows, PAGE))
    page_specs = lambda: [pl.BlockSpec((1, 1, D_B, PAGE),
                                       lambda s, pt, pg=pg: (layer, pt[s * n_pages + pg], 0, 0))
                          for pg in range(n_pages)]
    const = lambda shape: pl.BlockSpec(shape, lambda s, pt: (0,) * len(shape))
    grid_spec = pltpu.PrefetchScalarGridSpec(
        num_scalar_prefetch=1,
        grid=(n_seq,),
        in_specs=[pl.BlockSpec((1, 1, D_B), lambda s, pt: (s, 0, 0)),
                  const((rows, PAGE)), const((PAGE, PAGE)), const((rows, rows))]
                 + page_specs() + page_specs(),
        out_specs=pl.BlockSpec((1, 1, D_B), lambda s, pt: (s, 0, 0)),
    )
    out = pl.pallas_call(
        functools.partial(_attn_sample_kernel, n_pages),
        grid_spec=grid_spec,
        out_shape=jax.ShapeDtypeStruct((n_seq, 1, D_B), BF16),
        compiler_params=_params("arbitrary"),
        name="attn_sample",
    )(page_table.reshape(-1), q.reshape(n_seq, 1, D_B), bias_rows, _tri_keys_after(PAGE), later,
      *([cache_kt] * n_pages), *([cache_vt] * n_pages))
    return out.reshape(n_seq, D_B)


def _ssd_sample_prep_kernel(x_ref, st_ref, dt_ref, cw_ref, cb_ref, dtb_ref, alog_ref, e_ref,
                            nst_ref, xs_ref, xdt_t_ref, dec_t_ref, b_ref, c_ref):
    x = x_ref[...]
    hist = CONV_C_W - 1
    conv = cb_ref[...] + cw_ref[hist:hist + 1, :] * x
    for k in range(hist):
        conv = conv + cw_ref[k:k + 1, :] * st_ref[k]
    for k in range(hist - 1):
        nst_ref[k] = st_ref[k + 1]
    nst_ref[hist - 1] = x
    xbc = _silu(conv)
    xs = xbc[:, :D_INNER]
    xs_ref[...] = xs
    b_ref[...] = xbc[:, D_INNER:D_INNER + G_C * N_C]
    c_ref[...] = xbc[:, D_INNER + G_C * N_C:]
    dt = _softplus(dt_ref[...] + dtb_ref[...])
    e = e_ref[...]
    xdt_t_ref[...] = (xs * _spread(dt, e)).T
    dec_t_ref[...] = jnp.exp(_spread(dt * (-jnp.exp(alog_ref[...])), e)).T


def _ssd_sample_prep(xbc, state, dt, cw, cb, dtb, alog, e):
    n = xbc.shape[0]
    shapes = [(state.shape, F32), ((n, D_INNER), F32), ((D_INNER, n), F32), ((D_INNER, n), F32),
              ((n, G_C * N_C), F32), ((n, G_C * N_C), F32)]
    full = lambda s: pl.BlockSpec(s, lambda i: (0,) * len(s))
    return pl.pallas_call(
        _ssd_sample_prep_kernel,
        grid=(1,),
        in_specs=[full(xbc.shape), full(state.shape), full(dt.shape), full(cw.shape), full(cb.shape),
                  full(dtb.shape), full(alog.shape), full(e.shape)],
        out_specs=[full(s) for s, _ in shapes],
        out_shape=[jax.ShapeDtypeStruct(s, d) for s, d in shapes],
        compiler_params=_params("arbitrary"),
        name="ssd_sample_prep",
    )(xbc, state, dt, cw, cb, dtb, alog, e)


def _ssd_sample_state_kernel(xdt_t_ref, dec_t_ref, b_ref, c_ref, h_ref, *rest):
    nh_ref, y_t_ref = rest[-2:]
    n = xdt_t_ref.shape[1]
    per_step = h_ref.shape[0]
    rows = D_INNER // G_C

    @pl.when(pl.program_id(0) == 0)
    def _():
        y_t_ref[...] = jnp.zeros_like(y_t_ref)

    for jj in range(per_step):
        j = pl.program_id(0) * per_step + jj
        mine = lax.broadcasted_iota(jnp.int32, (rows, n), 1) == j
        for g in range(G_C):
            r0 = g * rows
            pick = lambda ref: jnp.sum(jnp.where(mine, ref[r0:r0 + rows, :], 0.0), axis=1, keepdims=True)
            h_new = (pick(dec_t_ref) * h_ref[jj, r0:r0 + rows, :]
                     + pick(xdt_t_ref) * b_ref[jj, :, g * N_C:(g + 1) * N_C])
            nh_ref[jj, r0:r0 + rows, :] = h_new
            y = jnp.sum(h_new * c_ref[jj, :, g * N_C:(g + 1) * N_C], axis=1, keepdims=True)
            y_t_ref[r0:r0 + rows, :] = jnp.where(mine, y, y_t_ref[r0:r0 + rows, :])


def _ssd_sample_state(xdt_t, dec_t, bm, cm, state, layer, prev_state, per_step):
    depth, n = state.shape[:2]
    full = lambda s: pl.BlockSpec(s, lambda j: (0,) * len(s))
    per_seq = pl.BlockSpec((per_step, 1, G_C * N_C), lambda j: (j, 0, 0))
    slab = pl.BlockSpec((None, per_step, D_INNER, N_C), lambda j: (layer, j, 0, 0))
    prev, prev_specs, aliases = _alias_prev(prev_state, 5, 0)
    return pl.pallas_call(
        _ssd_sample_state_kernel,
        grid=(n // per_step,),
        in_specs=[full(xdt_t.shape), full(dec_t.shape), per_seq, per_seq, slab] + prev_specs,
        out_specs=[slab, full((D_INNER, n))],
        out_shape=[jax.ShapeDtypeStruct(state.shape, F32), jax.ShapeDtypeStruct((D_INNER, n), F32)],
        input_output_aliases=aliases,
        compiler_params=_params("arbitrary"),
        name="ssd_sample_state",
    )(xdt_t, dec_t, bm.reshape(n, 1, -1), cm.reshape(n, 1, -1), state, *prev)


def _ssd_sample_out_kernel(y_t_ref, xs_ref, z_ref, dskip_ref, g_ref, c_ref):
    c_ref[...] = _gated_group_norm(y_t_ref[...].T, xs_ref[...], z_ref[...], dskip_ref[...],
                                   g_ref[...]).astype(BF16)


def _ssd_sample_out(y_t, xs, z, dskip, g):
    n = xs.shape[0]
    full = lambda s: pl.BlockSpec(s, lambda i: (0,) * len(s))
    return pl.pallas_call(
        _ssd_sample_out_kernel,
        grid=(1,),
        in_specs=[full(y_t.shape), full(xs.shape), full(z.shape), full(dskip.shape), full(g.shape)],
        out_specs=full((n, D_INNER)),
        out_shape=jax.ShapeDtypeStruct((n, D_INNER), BF16),
        compiler_params=_params("arbitrary"),
        name="ssd_sample_out",
    )(y_t, xs, z, dskip, g)


def _token_tile(t):
    for tile in (512, 256):
        if t % tile == 0:
            return tile
    return t


def _tri_keys_after(n):
    i = jnp.arange(n)
    return (i[:, None] >= i[None, :]).astype(BF16)


def kernel(x_prompt, x_sample, cache_k, cache_v, state_conv_a, state_conv_ssm, state_ssm, page_table,
           p_prompt, p_sample, g_mix, w_in, conv_a_w, conv_a_b, ln_a_g, ln_a_b, g_q, g_k, sb_bias,
           conv_ssm_w, conv_ssm_b, dt_bias, a_log, d_skip, g_ssm, w_out, g_ffn, w_ffn_gate, w_ffn_up,
           w_ffn_down, g_ple, w_ple_gate, w_ple):
    depth = w_in.shape[0]
    batch, seq, _ = x_prompt.shape
    n_seq = x_sample.shape[0]
    tp = batch * seq
    assert x_sample.shape[1] == 1
    assert seq % CONF_TILE == 0 and seq % ATT_TILE == 0 and seq % SSD_CHUNK == 0

    pad_lanes = lambda v: jnp.pad(v, ((0, 0), (0, LANES - v.shape[1])))
    seg = (jnp.arange(D_B)[:, None] // DH_B == jnp.arange(D_B)[None, :] // DH_B).astype(BF16)
    head_lanes = (jnp.arange(LANES)[:, None] == jnp.arange(D_INNER)[None, :] // P_C).astype(BF16)
    tril = (jnp.arange(SSD_CHUNK)[:, None] >= jnp.arange(SSD_CHUNK)[None, :]).astype(BF16)
    tri_att = _tri_keys_after(ATT_TILE).T

    n_pool = cache_k.shape[1]
    cache_kt = jnp.transpose(cache_k, (0, 1, 3, 4, 2)).reshape(depth, n_pool, D_B, PAGE)
    cache_vt = jnp.transpose(cache_v, (0, 1, 3, 4, 2)).reshape(depth, n_pool, D_B, PAGE)
    conv_a_taps = jnp.transpose(state_conv_a, (0, 2, 1, 3))
    conv_c_taps = jnp.transpose(state_conv_ssm, (0, 2, 1, 3))
    ssm_flat = state_ssm.reshape(depth, n_seq, D_INNER, N_C)

    h_p = x_prompt.reshape(tp, D_MODEL)
    h_s = x_sample.reshape(n_seq, D_MODEL)
    pe_p = p_prompt.reshape(depth, tp, D_PLE)
    pe_s = p_sample.reshape(depth, n_seq, D_PLE)
    w_in_b = jnp.pad(w_in, ((0, 0), (0, 0), (0, W_IN_COLS - w_in.shape[2]))).astype(BF16)
    wo, wg, wu, wd, wpg, wp = (w.astype(BF16) for w in (w_out, w_ffn_gate, w_ffn_up, w_ffn_down,
                                                        w_ple_gate, w_ple))
    tile_p, tile_s = _token_tile(seq), _token_tile(n_seq)
    seqs_per_step = 4 if n_seq % 4 == 0 else 1
    kv_p = kv_s = ssm_s = None
    outs = {name: [] for name in ("cap", "ccp", "ssp", "cas", "ccs")}
    for i in range(depth):
        g_mix_i = g_mix[i][None]
        gq = jnp.tile(g_q[i], H_B)[None]
        gk = jnp.tile(g_k[i], H_B)[None]
        conv_a_w_i = jnp.pad(conv_a_w[i], ((0, CONF_PAD - CONV_A_W), (0, 0)))
        conv_c_w_i = jnp.pad(conv_ssm_w[i], ((0, SUBLANES - CONV_C_W), (0, 0)))
        dtb = pad_lanes(dt_bias[i][None])
        alog = pad_lanes(a_log[i][None])
        dskip = jnp.repeat(d_skip[i], P_C)[None]
        bias2 = sb_bias[i] * LOG2E
        post_w = (wo, g_ffn[i][None], wg, wu, wd, g_ple[i][None], wpg, wp)

        a_in, q, kt, vt, kb, vb, z, xbc, dt = _in_proj(h_p, g_mix_i, w_in_b, gq, gk, seg, tile_p,
                                                       batch, i, kv_p)
        kv_p = (kt, vt)
        a_out, a_tail = _conf_prompt(a_in, conv_a_w_i, conv_a_b[i][None], ln_a_g[i][None],
                                     ln_a_b[i][None], batch)
        b_out = _attn_prompt(bias2, q, kb, vb, tri_att, batch)
        c_out, ssm_p = _ssd_prompt(xbc, z, dt, conv_c_w_i, conv_ssm_b[i][None], dtb, alog, dskip,
                                   g_ssm[i][None], tril, head_lanes, batch)
        h_p = _post(h_p, a_out, b_out, c_out, pe_p, *post_w, tile_p, i)
        outs["cap"].append(a_tail[:, CONF_PAD - (CONV_A_W - 1):])
        outs["ccp"].append(xbc.reshape(batch, seq, CONV_C_DIM)[:, seq - (CONV_C_W - 1):])
        outs["ssp"].append(ssm_p.reshape(batch, H_C, P_C, N_C))

        a_in, q, kt, vt, _, _, z, xbc, dt = _in_proj(h_s, g_mix_i, w_in_b, gq, gk, seg, tile_s,
                                                     1, i, kv_s)
        kv_s = (kt, vt)
        a_out, conv_a_new = _conf_sample(a_in, conv_a_taps[i], conv_a_w_i, conv_a_b[i][None],
                                         ln_a_g[i][None], ln_a_b[i][None], min(n_seq, 32))
        b_out = _attn_sample(page_table, q, bias2, cache_kt, cache_vt, i)
        conv_c_new, xs, xdt_t, dec_t, bm, cm = _ssd_sample_prep(
            xbc, conv_c_taps[i], dt, conv_c_w_i, conv_ssm_b[i][None], dtb, alog, head_lanes)
        ssm_s, y_t = _ssd_sample_state(xdt_t, dec_t, bm, cm, ssm_flat, i,
                                       None if ssm_s is None else (ssm_s,), seqs_per_step)
        c_out = _ssd_sample_out(y_t, xs, z, dskip, g_ssm[i][None])
        h_s = _post(h_s, a_out, b_out, c_out, pe_s, *post_w, tile_s, i)
        outs["cas"].append(jnp.transpose(conv_a_new, (1, 0, 2)))
        outs["ccs"].append(jnp.transpose(conv_c_new, (1, 0, 2)))

    st = lambda name: jnp.stack(outs[name])
    heads_last = lambda x: jnp.transpose(x.reshape(depth, x.shape[1], H_B, DH_B, x.shape[3]), (0, 1, 4, 2, 3))
    k_p, v_p = (heads_last(x) for x in kv_p)
    k_s, v_s = (jnp.transpose(heads_last(x), (0, 2, 1, 3, 4)) for x in kv_s)
    return (h_p.reshape(batch, seq, D_MODEL), h_s.reshape(n_seq, 1, D_MODEL),
            k_p, v_p, st("cap"), st("ccp"), st("ssp"),
            k_s, v_s, st("cas"), st("ccs"), ssm_s.reshape(depth, n_seq, H_C, P_C, N_C))
```

```python
import functools
import math

import jax
import jax.numpy as jnp
from jax import lax
from jax.experimental import pallas as pl
from jax.experimental.pallas import tpu as pltpu

F32 = jnp.float32
BF16 = jnp.bfloat16

D_MODEL = 1024
C_A = 512
CONV_A_W = 31
H_B = 8
DH_B = 64
D_B = H_B * DH_B
D_INNER = 1024
H_C = 16
P_C = 64
G_C = 2
N_C = 128
CONV_C_W = 4
CONV_C_DIM = D_INNER + 2 * G_C * N_C
D_FF = 2816
D_PLE = 256
EPS = 1e-6
PAGE = 128

LANES = 128
SUBLANES = 8
VMEM_LIMIT = 56 * 1024 * 1024

OFF_A = 0
OFF_Q = 2 * C_A
OFF_K = OFF_Q + D_B
OFF_V = OFF_K + D_B
OFF_Z = OFF_V + D_B
OFF_X = OFF_Z + D_INNER
OFF_DT = OFF_X + CONV_C_DIM
W_IN_COLS = OFF_DT + LANES

FF_CHUNK = 256
SSD_CHUNK = 128
CONF_TILE = 512
CONF_ROWS = 32
CONF_PAD = 32
ATT_TILE = 256
HEADS_PER_STEP = LANES // DH_B
LOG2E = 1.4426950408889634


def _const_spec(shape):
    zeros = (0,) * len(shape)
    return pl.BlockSpec(shape, lambda *_: zeros, pipeline_mode=pl.Buffered(1))


def _layer_spec(shape, layer):
    zeros = (0,) * len(shape)
    return pl.BlockSpec((None,) + tuple(shape), lambda *_: (layer,) + zeros, pipeline_mode=pl.Buffered(1))


def _alias_prev(prev, first_in, first_out):
    if prev is None:
        return (), [], {}
    specs = [pl.BlockSpec(memory_space=pl.ANY) for _ in prev]
    return tuple(prev), specs, {first_in + j: first_out + j for j in range(len(prev))}


def _params(*sem):
    return pltpu.CompilerParams(dimension_semantics=sem, vmem_limit_bytes=VMEM_LIMIT)


def _sigmoid(x):
    return 0.5 * jnp.tanh(0.5 * x) + 0.5


def _silu(x):
    return x * _sigmoid(x)


def _softplus(x):
    return jnp.maximum(x, 0.0) + jnp.log(1.0 + jnp.exp(-jnp.abs(x)))


def _softplus2(x):
    sign = jnp.uint32(0x80000000)
    neg_abs = lax.bitcast_convert_type(lax.bitcast_convert_type(x, jnp.uint32) | sign, F32)
    return jnp.maximum(x, 0.0) + jnp.log(1.0 + jnp.exp2(neg_abs)) * LOG2E


def _rms(x, g):
    return x * lax.rsqrt(jnp.mean(x * x, axis=-1, keepdims=True) + EPS) * g


def _dot(a, b):
    return jnp.dot(a, b, preferred_element_type=F32)


def _dot_nt(a, b):
    return lax.dot_general(a, b, (((1,), (1,)), ((), ())), preferred_element_type=F32)


def _three_parts(x):
    hi = x.astype(BF16)
    rest = x - hi.astype(F32)
    mid = rest.astype(BF16)
    return hi, mid, (rest - mid.astype(F32)).astype(BF16)


def _spread(x, m):
    hi, mid, lo = _three_parts(x)
    return _dot(hi, m) + _dot(mid, m) + _dot(lo, m)


def _collect(m, x):
    hi, mid, lo = _three_parts(x)
    return _dot(m, hi) + _dot(m, mid) + _dot(m, lo)


def _split_dot(x, m):
    hi = x.astype(BF16)
    lo = (x - hi.astype(F32)).astype(BF16)
    return _dot(hi, m) + _dot(lo, m)


def _in_proj_kernel(h_ref, g_ref, w_ref, gq_ref, gk_ref, seg_ref, *rest):
    a_ref, q_ref, k_ref, v_ref, kb_ref, vb_ref, z_ref, x_ref, dt_ref = rest[-9:]
    u = _rms(h_ref[...], g_ref[...]).astype(BF16)

    def proj(lo, hi):
        return _dot(u, w_ref[:, lo:hi])

    a_ref[...] = proj(OFF_A, OFF_Q)
    z_ref[...] = proj(OFF_Z, OFF_X)
    x_ref[...] = proj(OFF_X, OFF_DT)
    dt_ref[...] = proj(OFF_DT, W_IN_COLS)

    seg = seg_ref[...]

    def head_norm(t, g):
        ms = _split_dot(t * t, seg) * (1.0 / DH_B)
        return t * lax.rsqrt(ms + EPS) * g

    qn = head_norm(proj(OFF_Q, OFF_K), gq_ref[...])
    kn = head_norm(proj(OFF_K, OFF_V), gk_ref[...])
    v = proj(OFF_V, OFF_Z)
    q_ref[...] = (qn * (DH_B ** -0.5 * LOG2E)).astype(BF16)
    k_ref[0] = kn.T
    v_ref[0] = v.T
    kb_ref[...] = kn.astype(BF16)
    vb_ref[...] = v.astype(BF16)


def _in_proj(h, g_mix, w_in, gq, gk, seg, tm, batch, layer, prev_kv):
    t = h.shape[0]
    depth = w_in.shape[0]
    seq = t // batch
    nt = seq // tm
    row = lambda n: pl.BlockSpec((tm, n), lambda i: (i, 0))
    col = pl.BlockSpec((None, 1, D_B, tm), lambda i: (layer, i // nt, 0, i % nt))
    rows = lambda n, d: (row(n), jax.ShapeDtypeStruct((t, n), d))
    cols = (col, jax.ShapeDtypeStruct((depth, batch, D_B, seq), F32))
    outs = (rows(2 * C_A, F32), rows(D_B, BF16), cols, cols, rows(D_B, BF16), rows(D_B, BF16),
            rows(D_INNER, F32), rows(CONV_C_DIM, F32), rows(LANES, F32))
    prev, prev_specs, aliases = _alias_prev(prev_kv, 6, 2)
    return pl.pallas_call(
        _in_proj_kernel,
        grid=(t // tm,),
        in_specs=[row(D_MODEL), _const_spec((1, D_MODEL)), _layer_spec((D_MODEL, W_IN_COLS), layer),
                  _const_spec((1, D_B)), _const_spec((1, D_B)), _const_spec((D_B, D_B))] + prev_specs,
        out_specs=[spec for spec, _ in outs],
        out_shape=[shape for _, shape in outs],
        input_output_aliases=aliases,
        compiler_params=_params("parallel"),
        name="in_proj",
    )(h, g_mix, w_in, gq, gk, seg, *prev)


def _layer_norm_silu(x, g, b):
    mu = jnp.mean(x, axis=-1, keepdims=True)
    xc = x - mu
    y = xc * lax.rsqrt(jnp.mean(xc * xc, axis=-1, keepdims=True) + EPS)
    return _silu(y * g + b)


def _conf_prompt_kernel(a_ref, w_ref, b_ref, g_ref, bb_ref, o_ref, tail_ref, buf, shifted):
    tile = a_ref.shape[0]

    @pl.when(pl.program_id(1) == 0)
    def _():
        buf[0:CONF_PAD, :] = jnp.zeros((CONF_PAD, C_A), F32)

    a = a_ref[...]
    buf[CONF_PAD:CONF_PAD + tile, :] = a[:, :C_A] * _sigmoid(a[:, C_A:])
    n_shift = shifted.shape[1]
    for r in range(1, SUBLANES):
        shifted[r - 1] = buf[r:r + n_shift, :]
    first = CONF_PAD - (CONV_A_W - 1)
    for r0 in range(0, tile, CONF_ROWS):
        acc = jnp.zeros((CONF_ROWS, C_A), F32)
        for k in range(CONV_A_W):
            r = (first + k) % SUBLANES
            base = first + r0 + k - r
            rows = buf[base:base + CONF_ROWS, :] if r == 0 else shifted[r - 1, base:base + CONF_ROWS, :]
            acc = acc + w_ref[k:k + 1, :] * rows
        y = _layer_norm_silu(acc + b_ref[...], g_ref[...], bb_ref[...])
        o_ref[r0:r0 + CONF_ROWS, :] = y.astype(BF16)
    tail = buf[tile:tile + CONF_PAD, :]
    buf[0:CONF_PAD, :] = tail
    tail_ref[0] = tail


def _conf_prompt(a_in, w, b, g, bb, batch):
    t = a_in.shape[0]
    nt = t // batch // CONF_TILE
    return pl.pallas_call(
        _conf_prompt_kernel,
        grid=(batch, nt),
        in_specs=[pl.BlockSpec((CONF_TILE, 2 * C_A), lambda bi, ti: (bi * nt + ti, 0)),
                  _const_spec((CONF_PAD, C_A)), _const_spec((1, C_A)), _const_spec((1, C_A)),
                  _const_spec((1, C_A))],
        out_specs=[pl.BlockSpec((CONF_TILE, C_A), lambda bi, ti: (bi * nt + ti, 0)),
                   pl.BlockSpec((1, CONF_PAD, C_A), lambda bi, ti: (bi, 0, 0))],
        out_shape=[jax.ShapeDtypeStruct((t, C_A), BF16),
                   jax.ShapeDtypeStruct((batch, CONF_PAD, C_A), F32)],
        scratch_shapes=[pltpu.VMEM((CONF_PAD + CONF_TILE, C_A), F32),
                        pltpu.VMEM((SUBLANES - 1, CONF_PAD + CONF_TILE - SUBLANES, C_A), F32)],
        compiler_params=_params("parallel", "arbitrary"),
        name="conformer_prompt",
    )(a_in, w, b, g, bb)


def _attn_prompt_kernel(bias_ref, q_ref, k_ref, v_ref, tri_ref, o_ref, s_ref, cs_ref, acc_ref, carry_ref):
    qi = pl.program_id(1)
    tile = q_ref.shape[0]
    tri = tri_ref[...]
    lane = lax.broadcasted_iota(jnp.int32, (tile, LANES), 1)
    row = lax.broadcasted_iota(jnp.int32, (tile, tile), 0)
    col = lax.broadcasted_iota(jnp.int32, (tile, tile), 1)
    causal = col < row

    heads = []
    for lt in range(D_B // LANES):
        q2 = q_ref[:, lt * LANES:(lt + 1) * LANES].astype(F32)
        for hh in range(HEADS_PER_STEP):
            qh = jnp.where((lane // DH_B) == hh, q2, 0.0).astype(BF16)
            heads.append((lt, qh, bias_ref[lt * HEADS_PER_STEP + hh]))
    n = len(heads)

    def scores(i, start):
        lt, qh, bias = heads[i]
        s_ref[i] = _dot_nt(qh, k_ref[pl.ds(start, tile), lt * LANES:(lt + 1) * LANES]) + bias

    def later_sums(i, masked):
        sp = _softplus2(s_ref[i])
        if masked:
            sp = jnp.where(causal, sp, 0.0)
        cs_ref[i] = _dot(sp.astype(BF16), tri)

    def weigh(i, start, masked):
        lt = heads[i][0]
        cs = cs_ref[i]
        carry = carry_ref[i]
        w = jnp.exp2(s_ref[i] - cs - carry)
        if masked:
            w = jnp.where(causal, w, 0.0)
        acc_ref[i] += _dot(w.astype(BF16), v_ref[pl.ds(start, tile), lt * LANES:(lt + 1) * LANES])
        carry_ref[i] = carry + cs[:, 0:1]

    acc_ref[...] = jnp.zeros_like(acc_ref)
    carry_ref[...] = jnp.zeros_like(carry_ref)
    start0 = pl.multiple_of(qi * tile, tile)
    for i in range(n):
        scores(i, start0)
    for i in range(n):
        later_sums(i, True)
    nxt0 = pl.multiple_of(jnp.maximum(qi - 1, 0) * tile, tile)
    for i in range(n):
        weigh(i, start0, True)
        scores(i, nxt0)
    for i in range(n):
        later_sums(i, False)

    @pl.loop(0, qi - 1)
    def _(j):
        cur = qi - 1 - j
        for i in range(n):
            weigh(i, pl.multiple_of(cur * tile, tile), False)
            scores(i, pl.multiple_of((cur - 1) * tile, tile))
        for i in range(n):
            later_sums(i, False)

    @pl.when(qi > 0)
    def _():
        for i in range(n):
            weigh(i, 0, False)

    for lt in range(D_B // LANES):
        out = acc_ref[lt * HEADS_PER_STEP]
        for hh in range(1, HEADS_PER_STEP):
            out = jnp.where((lane // DH_B) == hh, acc_ref[lt * HEADS_PER_STEP + hh], out)
        o_ref[:, lt * LANES:(lt + 1) * LANES] = out.astype(BF16)


def _attn_prompt(bias, q, k, v, tri, batch):
    t = q.shape[0]
    seq = t // batch
    nq = seq // ATT_TILE
    resident = lambda: pl.BlockSpec((seq, D_B), lambda b, i, *_: (b, 0), pipeline_mode=pl.Buffered(1))
    grid_spec = pltpu.PrefetchScalarGridSpec(
        num_scalar_prefetch=1,
        grid=(batch, nq),
        in_specs=[pl.BlockSpec((ATT_TILE, D_B), lambda b, i, *_: (b * nq + i, 0)),
                  resident(), resident(),
                  pl.BlockSpec((ATT_TILE, ATT_TILE), lambda b, i, *_: (0, 0))],
        out_specs=pl.BlockSpec((ATT_TILE, D_B), lambda b, i, *_: (b * nq + i, 0)),
        scratch_shapes=[pltpu.VMEM((H_B, ATT_TILE, ATT_TILE), F32),
                        pltpu.VMEM((H_B, ATT_TILE, ATT_TILE), F32),
                        pltpu.VMEM((H_B, ATT_TILE, LANES), F32),
                        pltpu.VMEM((H_B, ATT_TILE, 1), F32)],
    )
    return pl.pallas_call(
        _attn_prompt_kernel,
        grid_spec=grid_spec,
        out_shape=jax.ShapeDtypeStruct((t, D_B), BF16),
        compiler_params=_params("parallel", "arbitrary"),
        name="attn_prompt",
    )(bias, q, k, v, tri)


def _gated_group_norm(y, xs, z, dskip, g):
    y = (y + dskip * xs) * _silu(z)
    gw = D_INNER // G_C
    return jnp.concatenate(
        [_rms(y[:, i * gw:(i + 1) * gw], g[:, i * gw:(i + 1) * gw]) for i in range(G_C)], axis=1)


def _ssd_prompt_kernel(x_ref, z_ref, dt_ref, cw_ref, cb_ref, dtb_ref, alog_ref, dskip_ref, g_ref,
                       tril_ref, e_ref, c_ref, st_ref, cbuf, ybuf):
    q = x_ref.shape[0]

    @pl.when(pl.program_id(1) == 0)
    def _():
        cbuf[0:SUBLANES, :] = jnp.zeros((SUBLANES, CONV_C_DIM), F32)
        st_ref[...] = jnp.zeros_like(st_ref)

    cbuf[SUBLANES:SUBLANES + q, :] = x_ref[...]
    first = SUBLANES - (CONV_C_W - 1)
    conv = cb_ref[...] + sum(cw_ref[k:k + 1, :] * cbuf[first + k:first + k + q, :]
                             for k in range(CONV_C_W))
    cbuf[0:SUBLANES, :] = cbuf[q:q + SUBLANES, :]
    xbc = _silu(conv)
    xs = xbc[:, :D_INNER]
    bm = xbc[:, D_INNER:D_INNER + G_C * N_C].astype(BF16)
    cm = xbc[:, D_INNER + G_C * N_C:].astype(BF16)

    dt = _softplus(dt_ref[...] + dtb_ref[...])
    a_cum = _collect(tril_ref[...], dt * (-jnp.exp(alog_ref[...])))
    e = e_ref[...]
    dt_x = _spread(dt, e)
    a_x = _spread(a_cum, e)
    a_cum_t = a_cum.T
    a_xt = a_x.T
    xdt = xs * dt_x
    exp_a_x = jnp.exp(a_x)

    row = lax.broadcasted_iota(jnp.int32, (q, q), 0)
    col = lax.broadcasted_iota(jnp.int32, (q, q), 1)
    causal = col <= row
    lane = lax.broadcasted_iota(jnp.int32, (q, LANES), 1)
    heads_per_group = H_C // G_C

    for g in range(G_C):
        bg = bm[:, g * N_C:(g + 1) * N_C]
        cg = cm[:, g * N_C:(g + 1) * N_C]
        cb = _dot_nt(cg, bg)
        for pp in range(heads_per_group // HEADS_PER_STEP):
            pair = g * (heads_per_group // HEADS_PER_STEP) + pp
            lo = pair * LANES
            xdt_p = xdt[:, lo:lo + LANES]
            xdt_pb = xdt_p.astype(BF16)
            y_p = None
            for hh in range(HEADS_PER_STEP):
                h = pair * HEADS_PER_STEP + hh
                seg = a_cum[:, h:h + 1] - a_cum_t[h:h + 1, :]
                m = (cb * jnp.exp(jnp.where(causal, seg, -jnp.inf))).astype(BF16)
                r = _dot(m, xdt_pb)
                y_p = r if y_p is None else jnp.where((lane // P_C) == hh, r, y_p)
            h_prev = st_ref[0, lo:lo + LANES, :]
            y_p = y_p + _dot_nt(cg, h_prev.astype(BF16)) * exp_a_x[:, lo:lo + LANES]
            ybuf[:, lo:lo + LANES] = y_p
            a_t = a_xt[lo:lo + LANES, :]
            tot = a_t[:, q - 1:q]
            s_new = _dot((xdt_p.T * jnp.exp(tot - a_t)).astype(BF16), bg)
            st_ref[0, lo:lo + LANES, :] = jnp.exp(tot) * h_prev + s_new

    c_ref[...] = _gated_group_norm(ybuf[...], xs, z_ref[...], dskip_ref[...], g_ref[...]).astype(BF16)


def _ssd_prompt(xbc, z, dt, cw, cb, dtb, alog, dskip, g, tril, e, batch):
    t = xbc.shape[0]
    nc = t // batch // SSD_CHUNK
    row = lambda n: pl.BlockSpec((SSD_CHUNK, n), lambda bi, ci: (bi * nc + ci, 0))
    return pl.pallas_call(
        _ssd_prompt_kernel,
        grid=(batch, nc),
        in_specs=[row(CONV_C_DIM), row(D_INNER), row(LANES),
                  _const_spec((SUBLANES, CONV_C_DIM)), _const_spec((1, CONV_C_DIM)),
                  _const_spec((1, LANES)), _const_spec((1, LANES)), _const_spec((1, D_INNER)),
                  _const_spec((1, D_INNER)), _const_spec((SSD_CHUNK, SSD_CHUNK)),
                  _const_spec((LANES, D_INNER))],
        out_specs=[row(D_INNER), pl.BlockSpec((1, D_INNER, N_C), lambda bi, ci: (bi, 0, 0))],
        out_shape=[jax.ShapeDtypeStruct((t, D_INNER), BF16),
                   jax.ShapeDtypeStruct((batch, D_INNER, N_C), F32)],
        scratch_shapes=[pltpu.VMEM((SUBLANES + SSD_CHUNK, CONV_C_DIM), F32),
                        pltpu.VMEM((SSD_CHUNK, D_INNER), F32)],
        compiler_params=_params("parallel", "arbitrary"),
        name="ssd_prompt",
    )(xbc, z, dt, cw, cb, dtb, alog, dskip, g, tril, e)


def _post_kernel(h_ref, a_ref, b_ref, c_ref, pe_ref, wo_ref, gf_ref, wg_ref, wu_ref, wd_ref,
                 gp_ref, wpg_ref, wp_ref, o_ref):
    h = (h_ref[...] + _dot(a_ref[...], wo_ref[0:C_A, :]) + _dot(b_ref[...], wo_ref[C_A:C_A + D_B, :])
         + _dot(c_ref[...], wo_ref[C_A + D_B:, :]))
    f = _rms(h, gf_ref[...]).astype(BF16)
    acc = jnp.zeros_like(h)
    for c0 in range(0, D_FF, FF_CHUNK):
        gate = _dot(f, wg_ref[:, c0:c0 + FF_CHUNK])
        up = _dot(f, wu_ref[:, c0:c0 + FF_CHUNK])
        acc = acc + _dot((_silu(gate) * up).astype(BF16), wd_ref[c0:c0 + FF_CHUNK, :])
    h = h + acc
    gate = _sigmoid(_dot(_rms(h, gp_ref[...]).astype(BF16), wpg_ref[...]))
    o_ref[...] = h + _dot(pe_ref[...].astype(BF16), wp_ref[...]) * gate


def _post(h, a, b, c, pe, wo, gf, wg, wu, wd, gp, wpg, wp, tm, layer):
    t = h.shape[0]
    row = lambda n: pl.BlockSpec((tm, n), lambda i: (i, 0))
    weight = lambda w: _layer_spec(w.shape[1:], layer)
    return pl.pallas_call(
        _post_kernel,
        grid=(t // tm,),
        in_specs=[row(D_MODEL), row(C_A), row(D_B), row(D_INNER),
                  pl.BlockSpec((None, tm, D_PLE), lambda i: (layer, i, 0)),
                  weight(wo), _const_spec((1, D_MODEL)), weight(wg), weight(wu), weight(wd),
                  _const_spec((1, D_MODEL)), weight(wpg), weight(wp)],
        out_specs=row(D_MODEL),
        out_shape=jax.ShapeDtypeStruct((t, D_MODEL), F32),
        compiler_params=_params("parallel"),
        name="post",
    )(h, a, b, c, pe, wo, gf, wg, wu, wd, gp, wpg, wp)


def _conf_sample_kernel(a_ref, st_ref, w_ref, b_ref, g_ref, bb_ref, o_ref, nst_ref):
    a = a_ref[...]
    glu = a[:, :C_A] * _sigmoid(a[:, C_A:])
    hist = CONV_A_W - 1
    acc = b_ref[...] + w_ref[hist:hist + 1, :] * glu
    for k in range(hist):
        acc = acc + w_ref[k:k + 1, :] * st_ref[k]
    o_ref[...] = _layer_norm_silu(acc, g_ref[...], bb_ref[...]).astype(BF16)
    for k in range(hist - 1):
        nst_ref[k] = st_ref[k + 1]
    nst_ref[hist - 1] = glu


def _conf_sample(a_in, state, w, b, g, bb, nb):
    n = a_in.shape[0]
    hist = CONV_A_W - 1
    row = lambda m: pl.BlockSpec((nb, m), lambda i: (i, 0))
    taps = pl.BlockSpec((hist, nb, C_A), lambda i: (0, i, 0))
    return pl.pallas_call(
        _conf_sample_kernel,
        grid=(n // nb,),
        in_specs=[row(2 * C_A), taps, _const_spec((CONF_PAD, C_A)), _const_spec((1, C_A)),
                  _const_spec((1, C_A)), _const_spec((1, C_A))],
        out_specs=[row(C_A), taps],
        out_shape=[jax.ShapeDtypeStruct((n, C_A), BF16), jax.ShapeDtypeStruct((hist, n, C_A), F32)],
        compiler_params=_params("parallel"),
        name="conformer_sample",
    )(a_in, state, w, b, g, bb)


def _attn_sample_kernel(n_pages, pt_ref, q_ref, bias_ref, tri_ref, later_ref, *refs):
    k_refs = refs[:n_pages]
    v_refs = refs[n_pages:2 * n_pages]
    o_ref = refs[2 * n_pages]
    own = (lax.broadcasted_iota(jnp.int32, (H_B, D_B), 1) // DH_B
           == lax.broadcasted_iota(jnp.int32, (H_B, D_B), 0))
    q_rows = jnp.broadcast_to(q_ref[0].astype(F32), (H_B, D_B))
    qmat = jnp.where(own, q_rows, 0.0).astype(BF16)
    s = jnp.concatenate([_dot(qmat, k_refs[pg][0, 0].astype(BF16)) for pg in range(n_pages)], axis=0)
    s = s + bias_ref[...]
    cs = _split_dot(_softplus2(s), tri_ref[...])
    page_tot = jnp.broadcast_to(cs[:, 0:1], cs.shape)
    cs = cs + _collect(later_ref[...], page_tot)
    w = jnp.exp2(s - cs).astype(BF16)
    acc = jnp.zeros((H_B, D_B), F32)
    for pg in range(n_pages):
        acc = acc + _dot_nt(w[pg * H_B:(pg + 1) * H_B, :], v_refs[pg][0, 0].astype(BF16))
    o_ref[0] = jnp.sum(jnp.where(own, acc, 0.0), axis=0, keepdims=True).astype(BF16)


def _attn_sample(page_table, q, bias, cache_kt, cache_vt, layer):
    n_seq, n_pages = page_table.shape
    rows = n_pages * H_B
    r = jnp.arange(rows)
    later = ((r[:, None] % H_B == r[None, :] % H_B) & (r[None, :] // H_B > r[:, None] // H_B)).astype(BF16)
    bias_rows = jnp.broadcast_to(jnp.tile(bias, n_pages)[:, None], (rows, PAGE))
    page_specs = lambda: [pl.BlockSpec((1, 1, D_B, PAGE),
                                       lambda s, pt, pg=pg: (layer, pt[s * n_pages + pg], 0, 0))
                          for pg in range(n_pages)]
    const = lambda shape: pl.BlockSpec(shape, lambda s, pt: (0,) * len(shape))
    grid_spec = pltpu.PrefetchScalarGridSpec(
        num_scalar_prefetch=1,
        grid=(n_seq,),
        in_specs=[pl.BlockSpec((1, 1, D_B), lambda s, pt: (s, 0, 0)),
                  const((rows, PAGE)), const((PAGE, PAGE)), const((rows, rows))]
                 + page_specs() + page_specs(),
        out_specs=pl.BlockSpec((1, 1, D_B), lambda s, pt: (s, 0, 0)),
    )
    out = pl.pallas_call(
        functools.partial(_attn_sample_kernel, n_pages),
        grid_spec=grid_spec,
        out_shape=jax.ShapeDtypeStruct((n_seq, 1, D_B), BF16),
        compiler_params=_params("arbitrary"),
        name="attn_sample",
    )(page_table.reshape(-1), q.reshape(n_seq, 1, D_B), bias_rows, _tri_keys_after(PAGE), later,
      *([cache_kt] * n_pages), *([cache_vt] * n_pages))
    return out.reshape(n_seq, D_B)


def _ssd_sample_prep_kernel(x_ref, st_ref, dt_ref, cw_ref, cb_ref, dtb_ref, alog_ref, e_ref,
                            nst_ref, xs_ref, xdt_t_ref, dec_t_ref, b_ref, c_ref):
    x = x_ref[...]
    hist = CONV_C_W - 1
    conv = cb_ref[...] + cw_ref[hist:hist + 1, :] * x
    for k in range(hist):
        conv = conv + cw_ref[k:k + 1, :] * st_ref[k]
    for k in range(hist - 1):
        nst_ref[k] = st_ref[k + 1]
    nst_ref[hist - 1] = x
    xbc = _silu(conv)
    xs = xbc[:, :D_INNER]
    xs_ref[...] = xs
    b_ref[...] = xbc[:, D_INNER:D_INNER + G_C * N_C]
    c_ref[...] = xbc[:, D_INNER + G_C * N_C:]
    dt = _softplus(dt_ref[...] + dtb_ref[...])
    e = e_ref[...]
    xdt_t_ref[...] = (xs * _spread(dt, e)).T
    dec_t_ref[...] = jnp.exp(_spread(dt * (-jnp.exp(alog_ref[...])), e)).T


def _ssd_sample_prep(xbc, state, dt, cw, cb, dtb, alog, e):
    n = xbc.shape[0]
    shapes = [(state.shape, F32), ((n, D_INNER), F32), ((D_INNER, n), F32), ((D_INNER, n), F32),
              ((n, G_C * N_C), F32), ((n, G_C * N_C), F32)]
    full = lambda s: pl.BlockSpec(s, lambda i: (0,) * len(s))
    return pl.pallas_call(
        _ssd_sample_prep_kernel,
        grid=(1,),
        in_specs=[full(xbc.shape), full(state.shape), full(dt.shape), full(cw.shape), full(cb.shape),
                  full(dtb.shape), full(alog.shape), full(e.shape)],
        out_specs=[full(s) for s, _ in shapes],
        out_shape=[jax.ShapeDtypeStruct(s, d) for s, d in shapes],
        compiler_params=_params("arbitrary"),
        name="ssd_sample_prep",
    )(xbc, state, dt, cw, cb, dtb, alog, e)


def _ssd_sample_state_kernel(xdt_t_ref, dec_t_ref, b_ref, c_ref, h_ref, *rest):
    nh_ref, y_t_ref = rest[-2:]
    n = xdt_t_ref.shape[1]
    per_step = h_ref.shape[0]
    rows = D_INNER // G_C

    @pl.when(pl.program_id(0) == 0)
    def _():
        y_t_ref[...] = jnp.zeros_like(y_t_ref)

    for jj in range(per_step):
        j = pl.program_id(0) * per_step + jj
        mine = lax.broadcasted_iota(jnp.int32, (rows, n), 1) == j
        for g in range(G_C):
            r0 = g * rows
            pick = lambda ref: jnp.sum(jnp.where(mine, ref[r0:r0 + rows, :], 0.0), axis=1, keepdims=True)
            h_new = (pick(dec_t_ref) * h_ref[jj, r0:r0 + rows, :]
                     + pick(xdt_t_ref) * b_ref[jj, :, g * N_C:(g + 1) * N_C])
            nh_ref[jj, r0:r0 + rows, :] = h_new
            y = jnp.sum(h_new * c_ref[jj, :, g * N_C:(g + 1) * N_C], axis=1, keepdims=True)
            y_t_ref[r0:r0 + rows, :] = jnp.where(mine, y, y_t_ref[r0:r0 + rows, :])


def _ssd_sample_state(xdt_t, dec_t, bm, cm, state, layer, prev_state, per_step):
    depth, n = state.shape[:2]
    full = lambda s: pl.BlockSpec(s, lambda j: (0,) * len(s))
    per_seq = pl.BlockSpec((per_step, 1, G_C * N_C), lambda j: (j, 0, 0))
    slab = pl.BlockSpec((None, per_step, D_INNER, N_C), lambda j: (layer, j, 0, 0))
    prev, prev_specs, aliases = _alias_prev(prev_state, 5, 0)
    return pl.pallas_call(
        _ssd_sample_state_kernel,
        grid=(n // per_step,),
        in_specs=[full(xdt_t.shape), full(dec_t.shape), per_seq, per_seq, slab] + prev_specs,
        out_specs=[slab, full((D_INNER, n))],
        out_shape=[jax.ShapeDtypeStruct(state.shape, F32), jax.ShapeDtypeStruct((D_INNER, n), F32)],
        input_output_aliases=aliases,
        compiler_params=_params("arbitrary"),
        name="ssd_sample_state",
    )(xdt_t, dec_t, bm.reshape(n, 1, -1), cm.reshape(n, 1, -1), state, *prev)


def _ssd_sample_out_kernel(y_t_ref, xs_ref, z_ref, dskip_ref, g_ref, c_ref):
    c_ref[...] = _gated_group_norm(y_t_ref[...].T, xs_ref[...], z_ref[...], dskip_ref[...],
                                   g_ref[...]).astype(BF16)


def _ssd_sample_out(y_t, xs, z, dskip, g):
    n = xs.shape[0]
    full = lambda s: pl.BlockSpec(s, lambda i: (0,) * len(s))
    return pl.pallas_call(
        _ssd_sample_out_kernel,
        grid=(1,),
        in_specs=[full(y_t.shape), full(xs.shape), full(z.shape), full(dskip.shape), full(g.shape)],
        out_specs=full((n, D_INNER)),
        out_shape=jax.ShapeDtypeStruct((n, D_INNER), BF16),
        compiler_params=_params("arbitrary"),
        name="ssd_sample_out",
    )(y_t, xs, z, dskip, g)


def _token_tile(t):
    for tile in (512, 256):
        if t % tile == 0:
            return tile
    return t


def _tri_keys_after(n):
    i = jnp.arange(n)
    return (i[:, None] >= i[None, :]).astype(BF16)


def kernel(x_prompt, x_sample, cache_k, cache_v, state_conv_a, state_conv_ssm, state_ssm, page_table,
           p_prompt, p_sample, g_mix, w_in, conv_a_w, conv_a_b, ln_a_g, ln_a_b, g_q, g_k, sb_bias,
           conv_ssm_w, conv_ssm_b, dt_bias, a_log, d_skip, g_ssm, w_out, g_ffn, w_ffn_gate, w_ffn_up,
           w_ffn_down, g_ple, w_ple_gate, w_ple):
    depth = w_in.shape[0]
    batch, seq, _ = x_prompt.shape
    n_seq = x_sample.shape[0]
    tp = batch * seq
    assert x_sample.shape[1] == 1
    assert seq % CONF_TILE == 0 and seq % ATT_TILE == 0 and seq % SSD_CHUNK == 0

    pad_lanes = lambda v: jnp.pad(v, ((0, 0), (0, LANES - v.shape[1])))
    seg = (jnp.arange(D_B)[:, None] // DH_B == jnp.arange(D_B)[None, :] // DH_B).astype(BF16)
    head_lanes = (jnp.arange(LANES)[:, None] == jnp.arange(D_INNER)[None, :] // P_C).astype(BF16)
    tril = (jnp.arange(SSD_CHUNK)[:, None] >= jnp.arange(SSD_CHUNK)[None, :]).astype(BF16)
    tri_att = _tri_keys_after(ATT_TILE)

    n_pool = cache_k.shape[1]
    cache_kt = jnp.transpose(cache_k, (0, 1, 3, 4, 2)).reshape(depth, n_pool, D_B, PAGE)
    cache_vt = jnp.transpose(cache_v, (0, 1, 3, 4, 2)).reshape(depth, n_pool, D_B, PAGE)
    conv_a_taps = jnp.transpose(state_conv_a, (0, 2, 1, 3))
    conv_c_taps = jnp.transpose(state_conv_ssm, (0, 2, 1, 3))
    ssm_flat = state_ssm.reshape(depth, n_seq, D_INNER, N_C)

    h_p = x_prompt.reshape(tp, D_MODEL)
    h_s = x_sample.reshape(n_seq, D_MODEL)
    pe_p = p_prompt.reshape(depth, tp, D_PLE)
    pe_s = p_sample.reshape(depth, n_seq, D_PLE)
    w_in_b = jnp.pad(w_in, ((0, 0), (0, 0), (0, W_IN_COLS - w_in.shape[2]))).astype(BF16)
    wo, wg, wu, wd, wpg, wp = (w.astype(BF16) for w in (w_out, w_ffn_gate, w_ffn_up, w_ffn_down,
                                                        w_ple_gate, w_ple))
    tile_p, tile_s = _token_tile(seq), _token_tile(n_seq)
    seqs_per_step = 4 if n_seq % 4 == 0 else 1
    kv_p = kv_s = ssm_s = None
    outs = {name: [] for name in ("cap", "ccp", "ssp", "cas", "ccs")}
    for i in range(depth):
        g_mix_i = g_mix[i][None]
        gq = jnp.tile(g_q[i], H_B)[None]
        gk = jnp.tile(g_k[i], H_B)[None]
        conv_a_w_i = jnp.pad(conv_a_w[i], ((0, CONF_PAD - CONV_A_W), (0, 0)))
        conv_c_w_i = jnp.pad(conv_ssm_w[i], ((0, SUBLANES - CONV_C_W), (0, 0)))
        dtb = pad_lanes(dt_bias[i][None])
        alog = pad_lanes(a_log[i][None])
        dskip = jnp.repeat(d_skip[i], P_C)[None]
        bias2 = sb_bias[i] * LOG2E
        post_w = (wo, g_ffn[i][None], wg, wu, wd, g_ple[i][None], wpg, wp)

        a_in, q, kt, vt, kb, vb, z, xbc, dt = _in_proj(h_p, g_mix_i, w_in_b, gq, gk, seg, tile_p,
                                                       batch, i, kv_p)
        kv_p = (kt, vt)
        a_out, a_tail = _conf_prompt(a_in, conv_a_w_i, conv_a_b[i][None], ln_a_g[i][None],
                                     ln_a_b[i][None], batch)
        b_out = _attn_prompt(bias2, q, kb, vb, tri_att, batch)
        c_out, ssm_p = _ssd_prompt(xbc, z, dt, conv_c_w_i, conv_ssm_b[i][None], dtb, alog, dskip,
                                   g_ssm[i][None], tril, head_lanes, batch)
        h_p = _post(h_p, a_out, b_out, c_out, pe_p, *post_w, tile_p, i)
        outs["cap"].append(a_tail[:, CONF_PAD - (CONV_A_W - 1):])
        outs["ccp"].append(xbc.reshape(batch, seq, CONV_C_DIM)[:, seq - (CONV_C_W - 1):])
        outs["ssp"].append(ssm_p.reshape(batch, H_C, P_C, N_C))

        a_in, q, kt, vt, _, _, z, xbc, dt = _in_proj(h_s, g_mix_i, w_in_b, gq, gk, seg, tile_s,
                                                     1, i, kv_s)
        kv_s = (kt, vt)
        a_out, conv_a_new = _conf_sample(a_in, conv_a_taps[i], conv_a_w_i, conv_a_b[i][None],
                                         ln_a_g[i][None], ln_a_b[i][None], min(n_seq, 32))
        b_out = _attn_sample(page_table, q, bias2, cache_kt, cache_vt, i)
        conv_c_new, xs, xdt_t, dec_t, bm, cm = _ssd_sample_prep(
            xbc, conv_c_taps[i], dt, conv_c_w_i, conv_ssm_b[i][None], dtb, alog, head_lanes)
        ssm_s, y_t = _ssd_sample_state(xdt_t, dec_t, bm, cm, ssm_flat, i,
                                       None if ssm_s is None else (ssm_s,), seqs_per_step)
        c_out = _ssd_sample_out(y_t, xs, z, dskip, g_ssm[i][None])
        h_s = _post(h_s, a_out, b_out, c_out, pe_s, *post_w, tile_s, i)
        outs["cas"].append(jnp.transpose(conv_a_new, (1, 0, 2)))
        outs["ccs"].append(jnp.transpose(conv_c_new, (1, 0, 2)))

    st = lambda name: jnp.stack(outs[name])
    heads_last = lambda x: jnp.transpose(x.reshape(depth, x.shape[1], H_B, DH_B, x.shape[3]), (0, 1, 4, 2, 3))
    k_p, v_p = (heads_last(x) for x in kv_p)
    k_s, v_s = (jnp.transpose(heads_last(x), (0, 2, 1, 3, 4)) for x in kv_s)
    return (h_p.reshape(batch, seq, D_MODEL), h_s.reshape(n_seq, 1, D_MODEL),
            k_p, v_p, st("cap"), st("ccp"), st("ssp"),
            k_s, v_s, st("cas"), st("ccs"), ssm_s.reshape(depth, n_seq, H_C, P_C, N_C))
```

```python
import functools
import math

import jax
import jax.numpy as jnp
from jax import lax
from jax.experimental import pallas as pl
from jax.experimental.pallas import tpu as pltpu

F32 = jnp.float32
BF16 = jnp.bfloat16

D_MODEL = 1024
C_A = 512
CONV_A_W = 31
H_B = 8
DH_B = 64
D_B = H_B * DH_B
D_INNER = 1024
H_C = 16
P_C = 64
G_C = 2
N_C = 128
CONV_C_W = 4
CONV_C_DIM = D_INNER + 2 * G_C * N_C
D_FF = 2816
D_PLE = 256
EPS = 1e-6
PAGE = 128

LANES = 128
SUBLANES = 8
VMEM_LIMIT = 56 * 1024 * 1024

OFF_A = 0
OFF_Q = 2 * C_A
OFF_K = OFF_Q + D_B
OFF_V = OFF_K + D_B
OFF_Z = OFF_V + D_B
OFF_X = OFF_Z + D_INNER
OFF_DT = OFF_X + CONV_C_DIM
W_IN_COLS = OFF_DT + LANES

FF_CHUNK = 256
SSD_CHUNK = 128
CONF_TILE = 512
CONF_ROWS = 32
CONF_PAD = 32
ATT_TILE = 256
HEADS_PER_STEP = LANES // DH_B
LOG2E = 1.4426950408889634


def _const_spec(shape):
    zeros = (0,) * len(shape)
    return pl.BlockSpec(shape, lambda *_: zeros, pipeline_mode=pl.Buffered(1))


def _layer_spec(shape, layer):
    zeros = (0,) * len(shape)
    return pl.BlockSpec((None,) + tuple(shape), lambda *_: (layer,) + zeros, pipeline_mode=pl.Buffered(1))


def _alias_prev(prev, first_in, first_out):
    if prev is None:
        return (), [], {}
    specs = [pl.BlockSpec(memory_space=pl.ANY) for _ in prev]
    return tuple(prev), specs, {first_in + j: first_out + j for j in range(len(prev))}


def _params(*sem):
    return pltpu.CompilerParams(dimension_semantics=sem, vmem_limit_bytes=VMEM_LIMIT)


def _sigmoid(x):
    return 0.5 * jnp.tanh(0.5 * x) + 0.5


def _silu(x):
    return x * _sigmoid(x)


def _softplus(x):
    return jnp.maximum(x, 0.0) + jnp.log(1.0 + jnp.exp(-jnp.abs(x)))


def _softplus2(x):
    sign = jnp.uint32(0x80000000)
    neg_abs = lax.bitcast_convert_type(lax.bitcast_convert_type(x, jnp.uint32) | sign, F32)
    return jnp.maximum(x, 0.0) + jnp.log(1.0 + jnp.exp2(neg_abs)) * LOG2E


def _rms(x, g):
    return x * lax.rsqrt(jnp.mean(x * x, axis=-1, keepdims=True) + EPS) * g


def _dot(a, b):
    return jnp.dot(a, b, preferred_element_type=F32)


def _dot_nt(a, b):
    return lax.dot_general(a, b, (((1,), (1,)), ((), ())), preferred_element_type=F32)


def _three_parts(x):
    hi = x.astype(BF16)
    rest = x - hi.astype(F32)
    mid = rest.astype(BF16)
    return hi, mid, (rest - mid.astype(F32)).astype(BF16)


def _spread(x, m):
    hi, mid, lo = _three_parts(x)
    return _dot(hi, m) + _dot(mid, m) + _dot(lo, m)


def _collect(m, x):
    hi, mid, lo = _three_parts(x)
    return _dot(m, hi) + _dot(m, mid) + _dot(m, lo)


def _split_dot(x, m):
    hi = x.astype(BF16)
    lo = (x - hi.astype(F32)).astype(BF16)
    return _dot(hi, m) + _dot(lo, m)


def _in_proj_kernel(h_ref, g_ref, w_ref, gq_ref, gk_ref, seg_ref, *rest):
    a_ref, q_ref, k_ref, v_ref, kb_ref, vb_ref, z_ref, x_ref, dt_ref = rest[-9:]
    u = _rms(h_ref[...], g_ref[...]).astype(BF16)

    def proj(lo, hi):
        return _dot(u, w_ref[:, lo:hi])

    a = proj(OFF_A, OFF_Q)
    a_ref[...] = a[:, :C_A] * _sigmoid(a[:, C_A:])
    z_ref[...] = proj(OFF_Z, OFF_X)
    dt_ref[...] = proj(OFF_DT, W_IN_COLS)
    x_ref[...] = proj(OFF_X, OFF_DT)

    seg = seg_ref[...]

    def head_norm(t, g):
        ms = _dot((t * t).astype(BF16), seg) * (1.0 / DH_B)
        return t * lax.rsqrt(ms + EPS) * g

    qn = head_norm(proj(OFF_Q, OFF_K), gq_ref[...])
    kn = head_norm(proj(OFF_K, OFF_V), gk_ref[...])
    v = proj(OFF_V, OFF_Z)
    q_ref[...] = (qn * (DH_B ** -0.5 * LOG2E)).astype(BF16)
    k_ref[0] = kn.T
    v_ref[0] = v.T
    kb_ref[...] = kn.astype(BF16)
    vb_ref[...] = v.astype(BF16)


def _in_proj(h, g_mix, w_in, gq, gk, seg, tm, batch, layer, prev_kv):
    t = h.shape[0]
    depth = w_in.shape[0]
    seq = t // batch
    nt = seq // tm
    row = lambda n: pl.BlockSpec((tm, n), lambda i: (i, 0))
    col = pl.BlockSpec((None, 1, D_B, tm), lambda i: (layer, i // nt, 0, i % nt))
    rows = lambda n, d: (row(n), jax.ShapeDtypeStruct((t, n), d))
    cols = (col, jax.ShapeDtypeStruct((depth, batch, D_B, seq), F32))
    outs = [rows(C_A, F32), rows(D_B, BF16), cols, cols, rows(D_B, BF16), rows(D_B, BF16),
            rows(D_INNER, F32), rows(CONV_C_DIM, F32), rows(LANES, F32)]
    prev, prev_specs, aliases = _alias_prev(prev_kv, 6, 2)
    return pl.pallas_call(
        _in_proj_kernel,
        grid=(t // tm,),
        in_specs=[row(D_MODEL), _const_spec((1, D_MODEL)), _layer_spec((D_MODEL, W_IN_COLS), layer),
                  _const_spec((1, D_B)), _const_spec((1, D_B)), _const_spec((D_B, D_B))] + prev_specs,
        out_specs=[spec for spec, _ in outs],
        out_shape=[shape for _, shape in outs],
        input_output_aliases=aliases,
        compiler_params=_params("parallel"),
        name="in_proj",
    )(h, g_mix, w_in, gq, gk, seg, *prev)


def _layer_norm_silu(x, g, b):
    mu = jnp.mean(x, axis=-1, keepdims=True)
    xc = x - mu
    y = xc * lax.rsqrt(jnp.mean(xc * xc, axis=-1, keepdims=True) + EPS)
    return _silu(y * g + b)


def _conf_prompt_kernel(a_ref, w_ref, b_ref, g_ref, bb_ref, o_ref, tail_ref, buf, shifted):
    tile = a_ref.shape[0]

    @pl.when(pl.program_id(1) == 0)
    def _():
        buf[0:CONF_PAD, :] = jnp.zeros((CONF_PAD, C_A), F32)

    buf[CONF_PAD:CONF_PAD + tile, :] = a_ref[...]
    n_shift = shifted.shape[1]
    for r in range(1, SUBLANES):
        shifted[r - 1] = buf[r:r + n_shift, :]
    first = CONF_PAD - (CONV_A_W - 1)
    for r0 in range(0, tile, CONF_ROWS):
        acc = jnp.zeros((CONF_ROWS, C_A), F32)
        for k in range(CONV_A_W):
            r = (first + k) % SUBLANES
            base = first + r0 + k - r
            rows = buf[base:base + CONF_ROWS, :] if r == 0 else shifted[r - 1, base:base + CONF_ROWS, :]
            acc = acc + w_ref[k:k + 1, :] * rows
        y = _layer_norm_silu(acc + b_ref[...], g_ref[...], bb_ref[...])
        o_ref[r0:r0 + CONF_ROWS, :] = y.astype(BF16)
    tail = buf[tile:tile + CONF_PAD, :]
    buf[0:CONF_PAD, :] = tail
    tail_ref[0] = tail


def _conf_prompt(glu, w, b, g, bb, batch):
    t = glu.shape[0]
    nt = t // batch // CONF_TILE
    return pl.pallas_call(
        _conf_prompt_kernel,
        grid=(batch, nt),
        in_specs=[pl.BlockSpec((CONF_TILE, C_A), lambda bi, ti: (bi * nt + ti, 0)),
                  _const_spec((CONF_PAD, C_A)), _const_spec((1, C_A)), _const_spec((1, C_A)),
                  _const_spec((1, C_A))],
        out_specs=[pl.BlockSpec((CONF_TILE, C_A), lambda bi, ti: (bi * nt + ti, 0)),
                   pl.BlockSpec((1, CONF_PAD, C_A), lambda bi, ti: (bi, 0, 0))],
        out_shape=[jax.ShapeDtypeStruct((t, C_A), BF16),
                   jax.ShapeDtypeStruct((batch, CONF_PAD, C_A), F32)],
        scratch_shapes=[pltpu.VMEM((CONF_PAD + CONF_TILE, C_A), F32),
                        pltpu.VMEM((SUBLANES - 1, CONF_PAD + CONF_TILE - SUBLANES, C_A), F32)],
        compiler_params=_params("parallel", "arbitrary"),
        name="conformer_prompt",
    )(glu, w, b, g, bb)


def _attn_prompt_kernel(bias_ref, q_ref, k_ref, v_ref, tri_ref, o_ref, s_ref, cs_ref, acc_ref, carry_ref):
    qi = pl.program_id(1)
    tile = q_ref.shape[0]
    tri = tri_ref[...]
    lane = lax.broadcasted_iota(jnp.int32, (tile, LANES), 1)
    row = lax.broadcasted_iota(jnp.int32, (tile, tile), 0)
    col = lax.broadcasted_iota(jnp.int32, (tile, tile), 1)
    causal = col < row

    heads = []
    for lt in range(D_B // LANES):
        q2 = q_ref[:, lt * LANES:(lt + 1) * LANES].astype(F32)
        for hh in range(HEADS_PER_STEP):
            qh = jnp.where((lane // DH_B) == hh, q2, 0.0).astype(BF16)
            heads.append((lt, qh, bias_ref[lt * HEADS_PER_STEP + hh]))
    n = len(heads)

    def scores(i, start):
        lt, qh, bias = heads[i]
        s_ref[i] = _dot_nt(qh, k_ref[pl.ds(start, tile), lt * LANES:(lt + 1) * LANES]) + bias

    def later_sums(i, masked):
        sp = _softplus2(s_ref[i])
        if masked:
            sp = jnp.where(causal, sp, 0.0)
        cs_ref[i] = _dot(sp.astype(BF16), tri)

    def weigh(i, start, masked):
        lt = heads[i][0]
        cs = cs_ref[i]
        carry = carry_ref[i]
        w = jnp.exp2(s_ref[i] - cs - carry)
        if masked:
            w = jnp.where(causal, w, 0.0)
        acc_ref[i] += _dot(w.astype(BF16), v_ref[pl.ds(start, tile), lt * LANES:(lt + 1) * LANES])
        carry_ref[i] = carry + cs[:, 0:1]

    acc_ref[...] = jnp.zeros_like(acc_ref)
    carry_ref[...] = jnp.zeros_like(carry_ref)
    start0 = pl.multiple_of(qi * tile, tile)
    for i in range(n):
        scores(i, start0)
    for i in range(n):
        later_sums(i, True)
    nxt0 = pl.multiple_of(jnp.maximum(qi - 1, 0) * tile, tile)
    for i in range(n):
        weigh(i, start0, True)
        scores(i, nxt0)
    for i in range(n):
        later_sums(i, False)

    @pl.loop(0, qi - 1)
    def _(j):
        cur = qi - 1 - j
        for i in range(n):
            weigh(i, pl.multiple_of(cur * tile, tile), False)
            scores(i, pl.multiple_of((cur - 1) * tile, tile))
        for i in range(n):
            later_sums(i, False)

    @pl.when(qi > 0)
    def _():
        for i in range(n):
            weigh(i, 0, False)

    for lt in range(D_B // LANES):
        out = acc_ref[lt * HEADS_PER_STEP]
        for hh in range(1, HEADS_PER_STEP):
            out = jnp.where((lane // DH_B) == hh, acc_ref[lt * HEADS_PER_STEP + hh], out)
        o_ref[:, lt * LANES:(lt + 1) * LANES] = out.astype(BF16)


def _attn_prompt(bias, q, k, v, tri, batch):
    t = q.shape[0]
    seq = t // batch
    nq = seq // ATT_TILE
    resident = lambda: pl.BlockSpec((seq, D_B), lambda b, i, *_: (b, 0), pipeline_mode=pl.Buffered(1))
    grid_spec = pltpu.PrefetchScalarGridSpec(
        num_scalar_prefetch=1,
        grid=(batch, nq),
        in_specs=[pl.BlockSpec((ATT_TILE, D_B), lambda b, i, *_: (b * nq + i, 0)),
                  resident(), resident(),
                  pl.BlockSpec((ATT_TILE, ATT_TILE), lambda b, i, *_: (0, 0))],
        out_specs=pl.BlockSpec((ATT_TILE, D_B), lambda b, i, *_: (b * nq + i, 0)),
        scratch_shapes=[pltpu.VMEM((H_B, ATT_TILE, ATT_TILE), F32),
                        pltpu.VMEM((H_B, ATT_TILE, ATT_TILE), F32),
                        pltpu.VMEM((H_B, ATT_TILE, LANES), F32),
                        pltpu.VMEM((H_B, ATT_TILE, 1), F32)],
    )
    return pl.pallas_call(
        _attn_prompt_kernel,
        grid_spec=grid_spec,
        out_shape=jax.ShapeDtypeStruct((t, D_B), BF16),
        compiler_params=_params("parallel", "arbitrary"),
        name="attn_prompt",
    )(bias, q, k, v, tri)


def _gated_group_norm(y, xs, z, dskip, g):
    y = (y + dskip * xs) * _silu(z)
    gw = D_INNER // G_C
    return jnp.concatenate(
        [_rms(y[:, i * gw:(i + 1) * gw], g[:, i * gw:(i + 1) * gw]) for i in range(G_C)], axis=1)


def _ssd_prompt_kernel(x_ref, z_ref, dt_ref, cw_ref, cb_ref, dtb_ref, alog_ref, dskip_ref, g_ref,
                       tril_ref, e_ref, c_ref, st_ref, cbuf, ybuf):
    q = x_ref.shape[0]

    @pl.when(pl.program_id(1) == 0)
    def _():
        cbuf[0:SUBLANES, :] = jnp.zeros((SUBLANES, CONV_C_DIM), F32)
        st_ref[...] = jnp.zeros_like(st_ref)

    cbuf[SUBLANES:SUBLANES + q, :] = x_ref[...]
    first = SUBLANES - (CONV_C_W - 1)
    conv = cb_ref[...] + sum(cw_ref[k:k + 1, :] * cbuf[first + k:first + k + q, :]
                             for k in range(CONV_C_W))
    cbuf[0:SUBLANES, :] = cbuf[q:q + SUBLANES, :]
    xbc = _silu(conv)
    xs = xbc[:, :D_INNER]
    bm = xbc[:, D_INNER:D_INNER + G_C * N_C].astype(BF16)
    cm = xbc[:, D_INNER + G_C * N_C:].astype(BF16)

    dt = _softplus(dt_ref[...] + dtb_ref[...])
    a_cum = _collect(tril_ref[...], dt * (-jnp.exp(alog_ref[...])))
    e = e_ref[...]
    dt_x = _spread(dt, e)
    a_x = _spread(a_cum, e)
    a_cum_t = a_cum.T
    a_xt = a_x.T
    xdt = xs * dt_x
    exp_a_x = jnp.exp(a_x)

    row = lax.broadcasted_iota(jnp.int32, (q, q), 0)
    col = lax.broadcasted_iota(jnp.int32, (q, q), 1)
    causal = col <= row
    lane = lax.broadcasted_iota(jnp.int32, (q, LANES), 1)
    heads_per_group = H_C // G_C

    for g in range(G_C):
        bg = bm[:, g * N_C:(g + 1) * N_C]
        cg = cm[:, g * N_C:(g + 1) * N_C]
        cb = _dot_nt(cg, bg)
        for pp in range(heads_per_group // HEADS_PER_STEP):
            pair = g * (heads_per_group // HEADS_PER_STEP) + pp
            lo = pair * LANES
            xdt_p = xdt[:, lo:lo + LANES]
            xdt_pb = xdt_p.astype(BF16)
            y_p = None
            for hh in range(HEADS_PER_STEP):
                h = pair * HEADS_PER_STEP + hh
                seg = a_cum[:, h:h + 1] - a_cum_t[h:h + 1, :]
                m = (cb * jnp.exp(jnp.where(causal, seg, -jnp.inf))).astype(BF16)
                r = _dot(m, xdt_pb)
                y_p = r if y_p is None else jnp.where((lane // P_C) == hh, r, y_p)
            h_prev = st_ref[0, lo:lo + LANES, :]
            y_p = y_p + _dot_nt(cg, h_prev.astype(BF16)) * exp_a_x[:, lo:lo + LANES]
            ybuf[:, lo:lo + LANES] = y_p
            a_t = a_xt[lo:lo + LANES, :]
            tot = a_t[:, q - 1:q]
            s_new = _dot((xdt_p.T * jnp.exp(tot - a_t)).astype(BF16), bg)
            st_ref[0, lo:lo + LANES, :] = jnp.exp(tot) * h_prev + s_new

    c_ref[...] = _gated_group_norm(ybuf[...], xs, z_ref[...], dskip_ref[...], g_ref[...]).astype(BF16)


def _ssd_prompt(xbc, z, dt, cw, cb, dtb, alog, dskip, g, tril, e, batch):
    t = xbc.shape[0]
    nc = t // batch // SSD_CHUNK
    row = lambda n: pl.BlockSpec((SSD_CHUNK, n), lambda bi, ci: (bi * nc + ci, 0))
    return pl.pallas_call(
        _ssd_prompt_kernel,
        grid=(batch, nc),
        in_specs=[row(CONV_C_DIM), row(D_INNER), row(LANES),
                  _const_spec((SUBLANES, CONV_C_DIM)), _const_spec((1, CONV_C_DIM)),
                  _const_spec((1, LANES)), _const_spec((1, LANES)), _const_spec((1, D_INNER)),
                  _const_spec((1, D_INNER)), _const_spec((SSD_CHUNK, SSD_CHUNK)),
                  _const_spec((LANES, D_INNER))],
        out_specs=[row(D_INNER), pl.BlockSpec((1, D_INNER, N_C), lambda bi, ci: (bi, 0, 0))],
        out_shape=[jax.ShapeDtypeStruct((t, D_INNER), BF16),
                   jax.ShapeDtypeStruct((batch, D_INNER, N_C), F32)],
        scratch_shapes=[pltpu.VMEM((SUBLANES + SSD_CHUNK, CONV_C_DIM), F32),
                        pltpu.VMEM((SSD_CHUNK, D_INNER), F32)],
        compiler_params=_params("parallel", "arbitrary"),
        name="ssd_prompt",
    )(xbc, z, dt, cw, cb, dtb, alog, dskip, g, tril, e)


def _post_kernel(h_ref, a_ref, b_ref, c_ref, pe_ref, wo_ref, gf_ref, wg_ref, wu_ref, wd_ref,
                 gp_ref, wpg_ref, wp_ref, o_ref):
    h = (h_ref[...] + _dot(a_ref[...], wo_ref[0:C_A, :]) + _dot(b_ref[...], wo_ref[C_A:C_A + D_B, :])
         + _dot(c_ref[...], wo_ref[C_A + D_B:, :]))
    f = _rms(h, gf_ref[...]).astype(BF16)
    acc = jnp.zeros_like(h)
    for c0 in range(0, D_FF, FF_CHUNK):
        gate = _dot(f, wg_ref[:, c0:c0 + FF_CHUNK])
        up = _dot(f, wu_ref[:, c0:c0 + FF_CHUNK])
        acc = acc + _dot((_silu(gate) * up).astype(BF16), wd_ref[c0:c0 + FF_CHUNK, :])
    h = h + acc
    gate = _sigmoid(_dot(_rms(h, gp_ref[...]).astype(BF16), wpg_ref[...]))
    o_ref[...] = h + _dot(pe_ref[...].astype(BF16), wp_ref[...]) * gate


def _post(h, a, b, c, pe, wo, gf, wg, wu, wd, gp, wpg, wp, tm, layer):
    t = h.shape[0]
    row = lambda n: pl.BlockSpec((tm, n), lambda i: (i, 0))
    weight = lambda w: _layer_spec(w.shape[1:], layer)
    return pl.pallas_call(
        _post_kernel,
        grid=(t // tm,),
        in_specs=[row(D_MODEL), row(C_A), row(D_B), row(D_INNER),
                  pl.BlockSpec((None, tm, D_PLE), lambda i: (layer, i, 0)),
                  weight(wo), _const_spec((1, D_MODEL)), weight(wg), weight(wu), weight(wd),
                  _const_spec((1, D_MODEL)), weight(wpg), weight(wp)],
        out_specs=row(D_MODEL),
        out_shape=jax.ShapeDtypeStruct((t, D_MODEL), F32),
        compiler_params=_params("parallel"),
        name="post",
    )(h, a, b, c, pe, wo, gf, wg, wu, wd, gp, wpg, wp)


def _conf_sample_kernel(a_ref, st_ref, w_ref, b_ref, g_ref, bb_ref, o_ref, nst_ref):
    glu = a_ref[...]
    hist = CONV_A_W - 1
    acc = b_ref[...] + w_ref[hist:hist + 1, :] * glu
    for k in range(hist):
        acc = acc + w_ref[k:k + 1, :] * st_ref[k]
    o_ref[...] = _layer_norm_silu(acc, g_ref[...], bb_ref[...]).astype(BF16)
    for k in range(hist - 1):
        nst_ref[k] = st_ref[k + 1]
    nst_ref[hist - 1] = glu


def _conf_sample(glu, state, w, b, g, bb, nb):
    n = glu.shape[0]
    hist = CONV_A_W - 1
    row = lambda m: pl.BlockSpec((nb, m), lambda i: (i, 0))
    taps = pl.BlockSpec((hist, nb, C_A), lambda i: (0, i, 0))
    return pl.pallas_call(
        _conf_sample_kernel,
        grid=(n // nb,),
        in_specs=[row(C_A), taps, _const_spec((CONF_PAD, C_A)), _const_spec((1, C_A)),
                  _const_spec((1, C_A)), _const_spec((1, C_A))],
        out_specs=[row(C_A), taps],
        out_shape=[jax.ShapeDtypeStruct((n, C_A), BF16), jax.ShapeDtypeStruct((hist, n, C_A), F32)],
        compiler_params=_params("parallel"),
        name="conformer_sample",
    )(glu, state, w, b, g, bb)


def _attn_sample_kernel(n_pages, pt_ref, q_ref, bias_ref, tri_ref, later_ref, *refs):
    k_refs = refs[:n_pages]
    v_refs = refs[n_pages:2 * n_pages]
    o_ref = refs[2 * n_pages]
    own = (lax.broadcasted_iota(jnp.int32, (H_B, D_B), 1) // DH_B
           == lax.broadcasted_iota(jnp.int32, (H_B, D_B), 0))
    q_rows = jnp.broadcast_to(q_ref[0].astype(F32), (H_B, D_B))
    qmat = jnp.where(own, q_rows, 0.0).astype(BF16)
    s = jnp.concatenate([_dot(qmat, k_refs[pg][0, 0].astype(BF16)) for pg in range(n_pages)], axis=0)
    s = s + bias_ref[...]
    cs = _split_dot(_softplus2(s), tri_ref[...])
    page_tot = jnp.broadcast_to(cs[:, 0:1], cs.shape)
    cs = cs + _collect(later_ref[...], page_tot)
    w = jnp.exp2(s - cs).astype(BF16)
    acc = jnp.zeros((H_B, D_B), F32)
    for pg in range(n_pages):
        acc = acc + _dot_nt(w[pg * H_B:(pg + 1) * H_B, :], v_refs[pg][0, 0].astype(BF16))
    o_ref[0] = jnp.sum(jnp.where(own, acc, 0.0), axis=0, keepdims=True).astype(BF16)


def _attn_sample(page_table, q, bias, cache_kt, cache_vt, layer):
    n_seq, n_pages = page_table.shape
    rows = n_pages * H_B
    r = jnp.arange(rows)
    later = ((r[:, None] % H_B == r[None, :] % H_B) & (r[None, :] // H_B > r[:, None] // H_B)).astype(BF16)
    bias_rows = jnp.broadcast_to(jnp.tile(bias, n_pages)[:, None], (rows, PAGE))
    page_specs = lambda: [pl.BlockSpec((1, 1, D_B, PAGE),
                                       lambda s, pt, pg=pg: (layer, pt[s * n_pages + pg], 0, 0))
                          for pg in range(n_pages)]
    const = lambda shape: pl.BlockSpec(shape, lambda s, pt: (0,) * len(shape))
    grid_spec = pltpu.PrefetchScalarGridSpec(
        num_scalar_prefetch=1,
        grid=(n_seq,),
        in_specs=[pl.BlockSpec((1, 1, D_B), lambda s, pt: (s, 0, 0)),
                  const((rows, PAGE)), const((PAGE, PAGE)), const((rows, rows))]
                 + page_specs() + page_specs(),
        out_specs=pl.BlockSpec((1, 1, D_B), lambda s, pt: (s, 0, 0)),
    )
    out = pl.pallas_call(
        functools.partial(_attn_sample_kernel, n_pages),
        grid_spec=grid_spec,
        out_shape=jax.ShapeDtypeStruct((n_seq, 1, D_B), BF16),
        compiler_params=_params("arbitrary"),
        name="attn_sample",
    )(page_table.reshape(-1), q.reshape(n_seq, 1, D_B), bias_rows, _tri_keys_after(PAGE), later,
      *([cache_kt] * n_pages), *([cache_vt] * n_pages))
    return out.reshape(n_seq, D_B)


def _ssd_sample_prep_kernel(x_ref, st_ref, dt_ref, cw_ref, cb_ref, dtb_ref, alog_ref, e_ref,
                            nst_ref, xs_ref, xdt_t_ref, dec_t_ref, b_ref, c_ref):
    x = x_ref[...]
    hist = CONV_C_W - 1
    conv = cb_ref[...] + cw_ref[hist:hist + 1, :] * x
    for k in range(hist):
        conv = conv + cw_ref[k:k + 1, :] * st_ref[k]
    for k in range(hist - 1):
        nst_ref[k] = st_ref[k + 1]
    nst_ref[hist - 1] = x
    xbc = _silu(conv)
    xs = xbc[:, :D_INNER]
    xs_ref[...] = xs
    b_ref[...] = xbc[:, D_INNER:D_INNER + G_C * N_C]
    c_ref[...] = xbc[:, D_INNER + G_C * N_C:]
    dt = _softplus(dt_ref[...] + dtb_ref[...])
    e = e_ref[...]
    xdt_t_ref[...] = (xs * _spread(dt, e)).T
    dec_t_ref[...] = jnp.exp(_spread(dt * (-jnp.exp(alog_ref[...])), e)).T


def _ssd_sample_prep(xbc, state, dt, cw, cb, dtb, alog, e):
    n = xbc.shape[0]
    shapes = [(state.shape, F32), ((n, D_INNER), F32), ((D_INNER, n), F32), ((D_INNER, n), F32),
              ((n, G_C * N_C), F32), ((n, G_C * N_C), F32)]
    full = lambda s: pl.BlockSpec(s, lambda i: (0,) * len(s))
    return pl.pallas_call(
        _ssd_sample_prep_kernel,
        grid=(1,),
        in_specs=[full(xbc.shape), full(state.shape), full(dt.shape), full(cw.shape), full(cb.shape),
                  full(dtb.shape), full(alog.shape), full(e.shape)],
        out_specs=[full(s) for s, _ in shapes],
        out_shape=[jax.ShapeDtypeStruct(s, d) for s, d in shapes],
        compiler_params=_params("arbitrary"),
        name="ssd_sample_prep",
    )(xbc, state, dt, cw, cb, dtb, alog, e)


def _ssd_sample_state_kernel(xdt_t_ref, dec_t_ref, b_ref, c_ref, h_ref, *rest):
    nh_ref, y_t_ref = rest[-2:]
    n = xdt_t_ref.shape[1]
    per_step = h_ref.shape[0]
    rows = D_INNER // G_C

    @pl.when(pl.program_id(0) == 0)
    def _():
        y_t_ref[...] = jnp.zeros_like(y_t_ref)

    for jj in range(per_step):
        j = pl.program_id(0) * per_step + jj
        mine = lax.broadcasted_iota(jnp.int32, (rows, n), 1) == j
        for g in range(G_C):
            r0 = g * rows
            pick = lambda ref: jnp.sum(jnp.where(mine, ref[r0:r0 + rows, :], 0.0), axis=1, keepdims=True)
            h_new = (pick(dec_t_ref) * h_ref[jj, r0:r0 + rows, :]
                     + pick(xdt_t_ref) * b_ref[jj, :, g * N_C:(g + 1) * N_C])
            nh_ref[jj, r0:r0 + rows, :] = h_new
            y = jnp.sum(h_new * c_ref[jj, :, g * N_C:(g + 1) * N_C], axis=1, keepdims=True)
            y_t_ref[r0:r0 + rows, :] = jnp.where(mine, y, y_t_ref[r0:r0 + rows, :])


def _ssd_sample_state(xdt_t, dec_t, bm, cm, state, layer, prev_state, per_step):
    depth, n = state.shape[:2]
    full = lambda s: pl.BlockSpec(s, lambda j: (0,) * len(s))
    per_seq = pl.BlockSpec((per_step, 1, G_C * N_C), lambda j: (j, 0, 0))
    slab = pl.BlockSpec((None, per_step, D_INNER, N_C), lambda j: (layer, j, 0, 0))
    prev, prev_specs, aliases = _alias_prev(prev_state, 5, 0)
    return pl.pallas_call(
        _ssd_sample_state_kernel,
        grid=(n // per_step,),
        in_specs=[full(xdt_t.shape), full(dec_t.shape), per_seq, per_seq, slab] + prev_specs,
        out_specs=[slab, full((D_INNER, n))],
        out_shape=[jax.ShapeDtypeStruct(state.shape, F32), jax.ShapeDtypeStruct((D_INNER, n), F32)],
        input_output_aliases=aliases,
        compiler_params=_params("arbitrary"),
        name="ssd_sample_state",
    )(xdt_t, dec_t, bm.reshape(n, 1, -1), cm.reshape(n, 1, -1), state, *prev)


def _ssd_sample_out_kernel(y_t_ref, xs_ref, z_ref, dskip_ref, g_ref, c_ref):
    c_ref[...] = _gated_group_norm(y_t_ref[...].T, xs_ref[...], z_ref[...], dskip_ref[...],
                                   g_ref[...]).astype(BF16)


def _ssd_sample_out(y_t, xs, z, dskip, g):
    n = xs.shape[0]
    full = lambda s: pl.BlockSpec(s, lambda i: (0,) * len(s))
    return pl.pallas_call(
        _ssd_sample_out_kernel,
        grid=(1,),
        in_specs=[full(y_t.shape), full(xs.shape), full(z.shape), full(dskip.shape), full(g.shape)],
        out_specs=full((n, D_INNER)),
        out_shape=jax.ShapeDtypeStruct((n, D_INNER), BF16),
        compiler_params=_params("arbitrary"),
        name="ssd_sample_out",
    )(y_t, xs, z, dskip, g)


def _token_tile(t):
    for tile in (512, 256):
        if t % tile == 0:
            return tile
    return t


def _tri_keys_after(n):
    i = jnp.arange(n)
    return (i[:, None] >= i[None, :]).astype(BF16)


def kernel(x_prompt, x_sample, cache_k, cache_v, state_conv_a, state_conv_ssm, state_ssm, page_table,
           p_prompt, p_sample, g_mix, w_in, conv_a_w, conv_a_b, ln_a_g, ln_a_b, g_q, g_k, sb_bias,
           conv_ssm_w, conv_ssm_b, dt_bias, a_log, d_skip, g_ssm, w_out, g_ffn, w_ffn_gate, w_ffn_up,
           w_ffn_down, g_ple, w_ple_gate, w_ple):
    depth = w_in.shape[0]
    batch, seq, _ = x_prompt.shape
    n_seq = x_sample.shape[0]
    tp = batch * seq
    assert x_sample.shape[1] == 1
    assert seq % CONF_TILE == 0 and seq % ATT_TILE == 0 and seq % SSD_CHUNK == 0

    pad_lanes = lambda v: jnp.pad(v, ((0, 0), (0, LANES - v.shape[1])))
    seg = (jnp.arange(D_B)[:, None] // DH_B == jnp.arange(D_B)[None, :] // DH_B).astype(BF16)
    head_lanes = (jnp.arange(LANES)[:, None] == jnp.arange(D_INNER)[None, :] // P_C).astype(BF16)
    tril = (jnp.arange(SSD_CHUNK)[:, None] >= jnp.arange(SSD_CHUNK)[None, :]).astype(BF16)
    tri_att = _tri_keys_after(ATT_TILE)

    n_pool = cache_k.shape[1]
    cache_kt = jnp.transpose(cache_k, (0, 1, 3, 4, 2)).reshape(depth, n_pool, D_B, PAGE)
    cache_vt = jnp.transpose(cache_v, (0, 1, 3, 4, 2)).reshape(depth, n_pool, D_B, PAGE)
    conv_a_taps = jnp.transpose(state_conv_a, (0, 2, 1, 3))
    conv_c_taps = jnp.transpose(state_conv_ssm, (0, 2, 1, 3))
    ssm_flat = state_ssm.reshape(depth, n_seq, D_INNER, N_C)

    h_p = x_prompt.reshape(tp, D_MODEL)
    h_s = x_sample.reshape(n_seq, D_MODEL)
    pe_p = p_prompt.reshape(depth, tp, D_PLE)
    pe_s = p_sample.reshape(depth, n_seq, D_PLE)
    w_in_b = jnp.pad(w_in, ((0, 0), (0, 0), (0, W_IN_COLS - w_in.shape[2]))).astype(BF16)
    wo, wg, wu, wd, wpg, wp = (w.astype(BF16) for w in (w_out, w_ffn_gate, w_ffn_up, w_ffn_down,
                                                        w_ple_gate, w_ple))
    tile_p, tile_s = _token_tile(seq), _token_tile(n_seq)
    seqs_per_step = 8 if n_seq % 8 == 0 else 1
    kv_p = kv_s = ssm_s = None
    outs = {name: [] for name in ("cap", "ccp", "ssp", "cas", "ccs")}
    for i in range(depth):
        g_mix_i = g_mix[i][None]
        gq = jnp.tile(g_q[i], H_B)[None]
        gk = jnp.tile(g_k[i], H_B)[None]
        conv_a_w_i = jnp.pad(conv_a_w[i], ((0, CONF_PAD - CONV_A_W), (0, 0)))
        conv_c_w_i = jnp.pad(conv_ssm_w[i], ((0, SUBLANES - CONV_C_W), (0, 0)))
        dtb = pad_lanes(dt_bias[i][None])
        alog = pad_lanes(a_log[i][None])
        dskip = jnp.repeat(d_skip[i], P_C)[None]
        bias2 = sb_bias[i] * LOG2E
        post_w = (wo, g_ffn[i][None], wg, wu, wd, g_ple[i][None], wpg, wp)

        glu, q, kt, vt, kb, vb, z, xbc, dt = _in_proj(h_p, g_mix_i, w_in_b, gq, gk, seg, tile_p,
                                                      batch, i, kv_p)
        kv_p = (kt, vt)
        a_out, a_tail = _conf_prompt(glu, conv_a_w_i, conv_a_b[i][None], ln_a_g[i][None],
                                     ln_a_b[i][None], batch)
        b_out = _attn_prompt(bias2, q, kb, vb, tri_att, batch)
        c_out, ssm_p = _ssd_prompt(xbc, z, dt, conv_c_w_i, conv_ssm_b[i][None], dtb, alog, dskip,
                                   g_ssm[i][None], tril, head_lanes, batch)
        h_p = _post(h_p, a_out, b_out, c_out, pe_p, *post_w, tile_p, i)
        outs["cap"].append(a_tail[:, CONF_PAD - (CONV_A_W - 1):])
        outs["ccp"].append(xbc.reshape(batch, seq, CONV_C_DIM)[:, seq - (CONV_C_W - 1):])
        outs["ssp"].append(ssm_p.reshape(batch, H_C, P_C, N_C))

        glu, q, kt, vt, _, _, z, xbc, dt = _in_proj(h_s, g_mix_i, w_in_b, gq, gk, seg, tile_s,
                                                    1, i, kv_s)
        kv_s = (kt, vt)
        a_out, conv_a_new = _conf_sample(glu, conv_a_taps[i], conv_a_w_i, conv_a_b[i][None],
                                         ln_a_g[i][None], ln_a_b[i][None], min(n_seq, 32))
        b_out = _attn_sample(page_table, q, bias2, cache_kt, cache_vt, i)
        conv_c_new, xs, xdt_t, dec_t, bm, cm = _ssd_sample_prep(
            xbc, conv_c_taps[i], dt, conv_c_w_i, conv_ssm_b[i][None], dtb, alog, head_lanes)
        ssm_s, y_t = _ssd_sample_state(xdt_t, dec_t, bm, cm, ssm_flat, i,
                                       None if ssm_s is None else (ssm_s,), seqs_per_step)
        c_out = _ssd_sample_out(y_t, xs, z, dskip, g_ssm[i][None])
        h_s = _post(h_s, a_out, b_out, c_out, pe_s, *post_w, tile_s, i)
        outs["cas"].append(jnp.transpose(conv_a_new, (1, 0, 2)))
        outs["ccs"].append(jnp.transpose(conv_c_new, (1, 0, 2)))

    st = lambda name: jnp.stack(outs[name])
    heads_last = lambda x: jnp.transpose(x.reshape(depth, x.shape[1], H_B, DH_B, x.shape[3]), (0, 1, 4, 2, 3))
    k_p, v_p = (heads_last(x) for x in kv_p)
    k_s, v_s = (jnp.transpose(heads_last(x), (0, 2, 1, 3, 4)) for x in kv_s)
    return (h_p.reshape(batch, seq, D_MODEL), h_s.reshape(n_seq, 1, D_MODEL),
            k_p, v_p, st("cap"), st("ccp"), st("ssp"),
            k_s, v_s, st("cas"), st("ccs"), ssm_s.reshape(depth, n_seq, H_C, P_C, N_C))
```

```python
import functools
import math

import jax
import jax.numpy as jnp
from jax import lax
from jax.experimental import pallas as pl
from jax.experimental.pallas import tpu as pltpu

F32 = jnp.float32
BF16 = jnp.bfloat16

D_MODEL = 1024
C_A = 512
CONV_A_W = 31
H_B = 8
DH_B = 64
D_B = H_B * DH_B
D_INNER = 1024
H_C = 16
P_C = 64
G_C = 2
N_C = 128
CONV_C_W = 4
CONV_C_DIM = D_INNER + 2 * G_C * N_C
D_FF = 2816
D_PLE = 256
EPS = 1e-6
PAGE = 128

LANES = 128
SUBLANES = 8
VMEM_LIMIT = 56 * 1024 * 1024

OFF_A = 0
OFF_Q = 2 * C_A
OFF_K = OFF_Q + D_B
OFF_V = OFF_K + D_B
OFF_Z = OFF_V + D_B
OFF_X = OFF_Z + D_INNER
OFF_DT = OFF_X + CONV_C_DIM
W_IN_COLS = OFF_DT + LANES

FF_CHUNK = 256
SSD_CHUNK = 128
CONF_TILE = 512
CONF_ROWS = 32
CONF_PAD = 32
ATT_TILE = 256
HEADS_PER_STEP = LANES // DH_B
LOG2E = 1.4426950408889634


def _const_spec(shape):
    zeros = (0,) * len(shape)
    return pl.BlockSpec(shape, lambda *_: zeros, pipeline_mode=pl.Buffered(1))


def _layer_spec(shape, layer):
    zeros = (0,) * len(shape)
    return pl.BlockSpec((None,) + tuple(shape), lambda *_: (layer,) + zeros, pipeline_mode=pl.Buffered(1))


def _alias_prev(prev, first_in, first_out):
    if prev is None:
        return (), [], {}
    specs = [pl.BlockSpec(memory_space=pl.ANY) for _ in prev]
    return tuple(prev), specs, {first_in + j: first_out + j for j in range(len(prev))}


def _params(*sem):
    return pltpu.CompilerParams(dimension_semantics=sem, vmem_limit_bytes=VMEM_LIMIT)


def _sigmoid(x):
    return 0.5 * jnp.tanh(0.5 * x) + 0.5


def _silu(x):
    return x * _sigmoid(x)


def _softplus(x):
    return jnp.maximum(x, 0.0) + jnp.log(1.0 + jnp.exp(-jnp.abs(x)))


def _softplus2(x):
    sign = jnp.uint32(0x80000000)
    neg_abs = lax.bitcast_convert_type(lax.bitcast_convert_type(x, jnp.uint32) | sign, F32)
    return jnp.maximum(x, 0.0) + jnp.log(1.0 + jnp.exp2(neg_abs)) * LOG2E


def _rms(x, g):
    return x * lax.rsqrt(jnp.mean(x * x, axis=-1, keepdims=True) + EPS) * g


def _dot(a, b):
    return jnp.dot(a, b, preferred_element_type=F32)


def _dot_nt(a, b):
    return lax.dot_general(a, b, (((1,), (1,)), ((), ())), preferred_element_type=F32)


def _three_parts(x):
    hi = x.astype(BF16)
    rest = x - hi.astype(F32)
    mid = rest.astype(BF16)
    return hi, mid, (rest - mid.astype(F32)).astype(BF16)


def _spread(x, m):
    hi, mid, lo = _three_parts(x)
    return _dot(hi, m) + _dot(mid, m) + _dot(lo, m)


def _collect(m, x):
    hi, mid, lo = _three_parts(x)
    return _dot(m, hi) + _dot(m, mid) + _dot(m, lo)


def _split_dot(x, m):
    hi = x.astype(BF16)
    lo = (x - hi.astype(F32)).astype(BF16)
    return _dot(hi, m) + _dot(lo, m)


def _in_proj_kernel(h_ref, g_ref, w_ref, gq_ref, gk_ref, seg_ref, *rest):
    a_ref, q_ref, k_ref, v_ref, kb_ref, vb_ref, z_ref, x_ref, dt_ref = rest[-9:]
    u = _rms(h_ref[...], g_ref[...]).astype(BF16)

    def proj(lo, hi):
        return _dot(u, w_ref[:, lo:hi])

    a = proj(OFF_A, OFF_Q)
    a_ref[...] = a[:, :C_A] * _sigmoid(a[:, C_A:])
    z_ref[...] = proj(OFF_Z, OFF_X)
    dt_ref[...] = proj(OFF_DT, W_IN_COLS)
    x_ref[...] = proj(OFF_X, OFF_DT)

    seg = seg_ref[...]

    def head_norm(t, g):
        ms = _dot((t * t).astype(BF16), seg) * (1.0 / DH_B)
        return t * lax.rsqrt(ms + EPS) * g

    qn = head_norm(proj(OFF_Q, OFF_K), gq_ref[...])
    kn = head_norm(proj(OFF_K, OFF_V), gk_ref[...])
    v = proj(OFF_V, OFF_Z)
    q_ref[...] = (qn * (DH_B ** -0.5 * LOG2E)).astype(BF16)
    k_ref[0] = kn.T
    v_ref[0] = v.T
    kb_ref[...] = kn.astype(BF16)
    vb_ref[...] = v.astype(BF16)


def _in_proj(h, g_mix, w_in, gq, gk, seg, tm, batch, layer, prev_kv):
    t = h.shape[0]
    depth = w_in.shape[0]
    seq = t // batch
    nt = seq // tm
    row = lambda n: pl.BlockSpec((tm, n), lambda i: (i, 0))
    col = pl.BlockSpec((None, 1, D_B, tm), lambda i: (layer, i // nt, 0, i % nt))
    rows = lambda n, d: (row(n), jax.ShapeDtypeStruct((t, n), d))
    cols = (col, jax.ShapeDtypeStruct((depth, batch, D_B, seq), F32))
    outs = [rows(C_A, F32), rows(D_B, BF16), cols, cols, rows(D_B, BF16), rows(D_B, BF16),
            rows(D_INNER, F32), rows(CONV_C_DIM, F32), rows(LANES, F32)]
    prev, prev_specs, aliases = _alias_prev(prev_kv, 6, 2)
    return pl.pallas_call(
        _in_proj_kernel,
        grid=(t // tm,),
        in_specs=[row(D_MODEL), _const_spec((1, D_MODEL)), _layer_spec((D_MODEL, W_IN_COLS), layer),
                  _const_spec((1, D_B)), _const_spec((1, D_B)), _const_spec((D_B, D_B))] + prev_specs,
        out_specs=[spec for spec, _ in outs],
        out_shape=[shape for _, shape in outs],
        input_output_aliases=aliases,
        compiler_params=_params("parallel"),
        name="in_proj",
    )(h, g_mix, w_in, gq, gk, seg, *prev)


def _layer_norm_silu(x, g, b):
    mu = jnp.mean(x, axis=-1, keepdims=True)
    xc = x - mu
    y = xc * lax.rsqrt(jnp.mean(xc * xc, axis=-1, keepdims=True) + EPS)
    return _silu(y * g + b)


def _conf_prompt_kernel(a_ref, w_ref, b_ref, g_ref, bb_ref, o_ref, tail_ref, buf, shifted):
    tile = a_ref.shape[0]

    @pl.when(pl.program_id(1) == 0)
    def _():
        buf[0:CONF_PAD, :] = jnp.zeros((CONF_PAD, C_A), F32)

    buf[CONF_PAD:CONF_PAD + tile, :] = a_ref[...]
    n_shift = shifted.shape[1]
    for r in range(1, SUBLANES):
        shifted[r - 1] = buf[r:r + n_shift, :]
    first = CONF_PAD - (CONV_A_W - 1)
    for r0 in range(0, tile, CONF_ROWS):
        acc = jnp.zeros((CONF_ROWS, C_A), F32)
        for k in range(CONV_A_W):
            r = (first + k) % SUBLANES
            base = first + r0 + k - r
            rows = buf[base:base + CONF_ROWS, :] if r == 0 else shifted[r - 1, base:base + CONF_ROWS, :]
            acc = acc + w_ref[k:k + 1, :] * rows
        y = _layer_norm_silu(acc + b_ref[...], g_ref[...], bb_ref[...])
        o_ref[r0:r0 + CONF_ROWS, :] = y.astype(BF16)
    tail = buf[tile:tile + CONF_PAD, :]
    buf[0:CONF_PAD, :] = tail
    tail_ref[0] = tail


def _conf_prompt(glu, w, b, g, bb, batch):
    t = glu.shape[0]
    nt = t // batch // CONF_TILE
    return pl.pallas_call(
        _conf_prompt_kernel,
        grid=(batch, nt),
        in_specs=[pl.BlockSpec((CONF_TILE, C_A), lambda bi, ti: (bi * nt + ti, 0)),
                  _const_spec((CONF_PAD, C_A)), _const_spec((1, C_A)), _const_spec((1, C_A)),
                  _const_spec((1, C_A))],
        out_specs=[pl.BlockSpec((CONF_TILE, C_A), lambda bi, ti: (bi * nt + ti, 0)),
                   pl.BlockSpec((1, CONF_PAD, C_A), lambda bi, ti: (bi, 0, 0))],
        out_shape=[jax.ShapeDtypeStruct((t, C_A), BF16),
                   jax.ShapeDtypeStruct((batch, CONF_PAD, C_A), F32)],
        scratch_shapes=[pltpu.VMEM((CONF_PAD + CONF_TILE, C_A), F32),
                        pltpu.VMEM((SUBLANES - 1, CONF_PAD + CONF_TILE - SUBLANES, C_A), F32)],
        compiler_params=_params("parallel", "arbitrary"),
        name="conformer_prompt",
    )(glu, w, b, g, bb)


def _attn_prompt_kernel(bias_ref, q_ref, k_ref, v_ref, tri_ref, o_ref, s_ref, cs_ref, acc_ref, carry_ref):
    qi = pl.program_id(1)
    tile = q_ref.shape[0]
    tri = tri_ref[...]
    lane = lax.broadcasted_iota(jnp.int32, (tile, LANES), 1)
    row = lax.broadcasted_iota(jnp.int32, (tile, tile), 0)
    col = lax.broadcasted_iota(jnp.int32, (tile, tile), 1)
    causal = col < row

    heads = []
    for lt in range(D_B // LANES):
        q2 = q_ref[:, lt * LANES:(lt + 1) * LANES].astype(F32)
        for hh in range(HEADS_PER_STEP):
            qh = jnp.where((lane // DH_B) == hh, q2, 0.0).astype(BF16)
            heads.append((lt, qh, bias_ref[lt * HEADS_PER_STEP + hh]))
    n = len(heads)

    def scores(i, start):
        lt, qh, bias = heads[i]
        s_ref[i] = _dot_nt(qh, k_ref[pl.ds(start, tile), lt * LANES:(lt + 1) * LANES]) + bias

    def later_sums(i, masked):
        sp = _softplus2(s_ref[i])
        if masked:
            sp = jnp.where(causal, sp, 0.0)
        cs_ref[i] = _dot(sp.astype(BF16), tri)

    def weigh(i, start, masked):
        lt = heads[i][0]
        cs = cs_ref[i]
        carry = carry_ref[i]
        w = jnp.exp2(s_ref[i] - cs - carry)
        if masked:
            w = jnp.where(causal, w, 0.0)
        acc_ref[i] += _dot(w.astype(BF16), v_ref[pl.ds(start, tile), lt * LANES:(lt + 1) * LANES])
        carry_ref[i] = carry + cs[:, 0:1]

    acc_ref[...] = jnp.zeros_like(acc_ref)
    carry_ref[...] = jnp.zeros_like(carry_ref)
    start0 = pl.multiple_of(qi * tile, tile)
    for i in range(n):
        scores(i, start0)
    for i in range(n):
        later_sums(i, True)
    nxt0 = pl.multiple_of(jnp.maximum(qi - 1, 0) * tile, tile)
    for i in range(n):
        weigh(i, start0, True)
        scores(i, nxt0)
    for i in range(n):
        later_sums(i, False)

    @pl.loop(0, qi - 1)
    def _(j):
        cur = qi - 1 - j
        for i in range(n):
            weigh(i, pl.multiple_of(cur * tile, tile), False)
            scores(i, pl.multiple_of((cur - 1) * tile, tile))
        for i in range(n):
            later_sums(i, False)

    @pl.when(qi > 0)
    def _():
        for i in range(n):
            weigh(i, 0, False)

    for lt in range(D_B // LANES):
        out = acc_ref[lt * HEADS_PER_STEP]
        for hh in range(1, HEADS_PER_STEP):
            out = jnp.where((lane // DH_B) == hh, acc_ref[lt * HEADS_PER_STEP + hh], out)
        o_ref[:, lt * LANES:(lt + 1) * LANES] = out.astype(BF16)


def _attn_prompt(bias, q, k, v, tri, batch):
    t = q.shape[0]
    seq = t // batch
    nq = seq // ATT_TILE
    resident = lambda: pl.BlockSpec((seq, D_B), lambda b, i, *_: (b, 0), pipeline_mode=pl.Buffered(1))
    grid_spec = pltpu.PrefetchScalarGridSpec(
        num_scalar_prefetch=1,
        grid=(batch, nq),
        in_specs=[pl.BlockSpec((ATT_TILE, D_B), lambda b, i, *_: (b * nq + i, 0)),
                  resident(), resident(),
                  pl.BlockSpec((ATT_TILE, ATT_TILE), lambda b, i, *_: (0, 0))],
        out_specs=pl.BlockSpec((ATT_TILE, D_B), lambda b, i, *_: (b * nq + i, 0)),
        scratch_shapes=[pltpu.VMEM((H_B, ATT_TILE, ATT_TILE), F32),
                        pltpu.VMEM((H_B, ATT_TILE, ATT_TILE), F32),
                        pltpu.VMEM((H_B, ATT_TILE, LANES), F32),
                        pltpu.VMEM((H_B, ATT_TILE, 1), F32)],
    )
    return pl.pallas_call(
        _attn_prompt_kernel,
        grid_spec=grid_spec,
        out_shape=jax.ShapeDtypeStruct((t, D_B), BF16),
        compiler_params=_params("parallel", "arbitrary"),
        name="attn_prompt",
    )(bias, q, k, v, tri)


def _gated_group_norm(y, xs, z, dskip, g):
    y = (y + dskip * xs) * _silu(z)
    gw = D_INNER // G_C
    return jnp.concatenate(
        [_rms(y[:, i * gw:(i + 1) * gw], g[:, i * gw:(i + 1) * gw]) for i in range(G_C)], axis=1)


def _ssd_prompt_kernel(x_ref, z_ref, dt_ref, cw_ref, cb_ref, dtb_ref, alog_ref, dskip_ref, g_ref,
                       tril_ref, e_ref, c_ref, st_ref, cbuf, ybuf):
    q = x_ref.shape[0]

    @pl.when(pl.program_id(1) == 0)
    def _():
        cbuf[0:SUBLANES, :] = jnp.zeros((SUBLANES, CONV_C_DIM), F32)
        st_ref[...] = jnp.zeros_like(st_ref)

    cbuf[SUBLANES:SUBLANES + q, :] = x_ref[...]
    first = SUBLANES - (CONV_C_W - 1)
    conv = cb_ref[...] + sum(cw_ref[k:k + 1, :] * cbuf[first + k:first + k + q, :]
                             for k in range(CONV_C_W))
    cbuf[0:SUBLANES, :] = cbuf[q:q + SUBLANES, :]
    xbc = _silu(conv)
    xs = xbc[:, :D_INNER]
    bm = xbc[:, D_INNER:D_INNER + G_C * N_C].astype(BF16)
    cm = xbc[:, D_INNER + G_C * N_C:].astype(BF16)

    dt = _softplus(dt_ref[...] + dtb_ref[...])
    a_cum = _collect(tril_ref[...], dt * (-jnp.exp(alog_ref[...])))
    e = e_ref[...]
    dt_x = _spread(dt, e)
    a_x = _spread(a_cum, e)
    a_cum_t = a_cum.T
    a_xt = a_x.T
    xdt = xs * dt_x
    exp_a_x = jnp.exp(a_x)

    row = lax.broadcasted_iota(jnp.int32, (q, q), 0)
    col = lax.broadcasted_iota(jnp.int32, (q, q), 1)
    causal = col <= row
    lane = lax.broadcasted_iota(jnp.int32, (q, LANES), 1)
    heads_per_group = H_C // G_C

    for g in range(G_C):
        bg = bm[:, g * N_C:(g + 1) * N_C]
        cg = cm[:, g * N_C:(g + 1) * N_C]
        cb = _dot_nt(cg, bg)
        for pp in range(heads_per_group // HEADS_PER_STEP):
            pair = g * (heads_per_group // HEADS_PER_STEP) + pp
            lo = pair * LANES
            xdt_p = xdt[:, lo:lo + LANES]
            xdt_pb = xdt_p.astype(BF16)
            y_p = None
            for hh in range(HEADS_PER_STEP):
                h = pair * HEADS_PER_STEP + hh
                seg = a_cum[:, h:h + 1] - a_cum_t[h:h + 1, :]
                m = (cb * jnp.exp(jnp.where(causal, seg, -jnp.inf))).astype(BF16)
                r = _dot(m, xdt_pb)
                y_p = r if y_p is None else jnp.where((lane // P_C) == hh, r, y_p)
            h_prev = st_ref[0, lo:lo + LANES, :]
            y_p = y_p + _dot_nt(cg, h_prev.astype(BF16)) * exp_a_x[:, lo:lo + LANES]
            ybuf[:, lo:lo + LANES] = y_p
            a_t = a_xt[lo:lo + LANES, :]
            tot = a_t[:, q - 1:q]
            s_new = _dot((xdt_p.T * jnp.exp(tot - a_t)).astype(BF16), bg)
            st_ref[0, lo:lo + LANES, :] = jnp.exp(tot) * h_prev + s_new

    c_ref[...] = _gated_group_norm(ybuf[...], xs, z_ref[...], dskip_ref[...], g_ref[...]).astype(BF16)


def _ssd_prompt(xbc, z, dt, cw, cb, dtb, alog, dskip, g, tril, e, batch):
    t = xbc.shape[0]
    nc = t // batch // SSD_CHUNK
    row = lambda n: pl.BlockSpec((SSD_CHUNK, n), lambda bi, ci: (bi * nc + ci, 0))
    return pl.pallas_call(
        _ssd_prompt_kernel,
        grid=(batch, nc),
        in_specs=[row(CONV_C_DIM), row(D_INNER), row(LANES),
                  _const_spec((SUBLANES, CONV_C_DIM)), _const_spec((1, CONV_C_DIM)),
                  _const_spec((1, LANES)), _const_spec((1, LANES)), _const_spec((1, D_INNER)),
                  _const_spec((1, D_INNER)), _const_spec((SSD_CHUNK, SSD_CHUNK)),
                  _const_spec((LANES, D_INNER))],
        out_specs=[row(D_INNER), pl.BlockSpec((1, D_INNER, N_C), lambda bi, ci: (bi, 0, 0))],
        out_shape=[jax.ShapeDtypeStruct((t, D_INNER), BF16),
                   jax.ShapeDtypeStruct((batch, D_INNER, N_C), F32)],
        scratch_shapes=[pltpu.VMEM((SUBLANES + SSD_CHUNK, CONV_C_DIM), F32),
                        pltpu.VMEM((SSD_CHUNK, D_INNER), F32)],
        compiler_params=_params("parallel", "arbitrary"),
        name="ssd_prompt",
    )(xbc, z, dt, cw, cb, dtb, alog, dskip, g, tril, e)


def _post_kernel(h_ref, a_ref, b_ref, c_ref, pe_ref, wo_ref, gf_ref, wg_ref, wu_ref, wd_ref,
                 gp_ref, wpg_ref, wp_ref, o_ref):
    h = (h_ref[...] + _dot(a_ref[...], wo_ref[0:C_A, :]) + _dot(b_ref[...], wo_ref[C_A:C_A + D_B, :])
         + _dot(c_ref[...], wo_ref[C_A + D_B:, :]))
    f = _rms(h, gf_ref[...]).astype(BF16)
    acc = jnp.zeros_like(h)
    for c0 in range(0, D_FF, FF_CHUNK):
        gate = _dot(f, wg_ref[:, c0:c0 + FF_CHUNK])
        up = _dot(f, wu_ref[:, c0:c0 + FF_CHUNK])
        acc = acc + _dot((_silu(gate) * up).astype(BF16), wd_ref[c0:c0 + FF_CHUNK, :])
    h = h + acc
    gate = _sigmoid(_dot(_rms(h, gp_ref[...]).astype(BF16), wpg_ref[...]))
    o_ref[...] = h + _dot(pe_ref[...].astype(BF16), wp_ref[...]) * gate


def _post(h, a, b, c, pe, wo, gf, wg, wu, wd, gp, wpg, wp, tm, layer):
    t = h.shape[0]
    row = lambda n: pl.BlockSpec((tm, n), lambda i: (i, 0))
    weight = lambda w: _layer_spec(w.shape[1:], layer)
    return pl.pallas_call(
        _post_kernel,
        grid=(t // tm,),
        in_specs=[row(D_MODEL), row(C_A), row(D_B), row(D_INNER),
                  pl.BlockSpec((None, tm, D_PLE), lambda i: (layer, i, 0)),
                  weight(wo), _const_spec((1, D_MODEL)), weight(wg), weight(wu), weight(wd),
                  _const_spec((1, D_MODEL)), weight(wpg), weight(wp)],
        out_specs=row(D_MODEL),
        out_shape=jax.ShapeDtypeStruct((t, D_MODEL), F32),
        compiler_params=_params("parallel"),
        name="post",
    )(h, a, b, c, pe, wo, gf, wg, wu, wd, gp, wpg, wp)


def _conf_sample_kernel(a_ref, st_ref, w_ref, b_ref, g_ref, bb_ref, o_ref, nst_ref):
    glu = a_ref[...]
    hist = CONV_A_W - 1
    acc = b_ref[...] + w_ref[hist:hist + 1, :] * glu
    for k in range(hist):
        acc = acc + w_ref[k:k + 1, :] * st_ref[k]
    o_ref[...] = _layer_norm_silu(acc, g_ref[...], bb_ref[...]).astype(BF16)
    for k in range(hist - 1):
        nst_ref[k] = st_ref[k + 1]
    nst_ref[hist - 1] = glu


def _conf_sample(glu, state, w, b, g, bb, nb):
    n = glu.shape[0]
    hist = CONV_A_W - 1
    row = lambda m: pl.BlockSpec((nb, m), lambda i: (i, 0))
    taps = pl.BlockSpec((hist, nb, C_A), lambda i: (0, i, 0))
    return pl.pallas_call(
        _conf_sample_kernel,
        grid=(n // nb,),
        in_specs=[row(C_A), taps, _const_spec((CONF_PAD, C_A)), _const_spec((1, C_A)),
                  _const_spec((1, C_A)), _const_spec((1, C_A))],
        out_specs=[row(C_A), taps],
        out_shape=[jax.ShapeDtypeStruct((n, C_A), BF16), jax.ShapeDtypeStruct((hist, n, C_A), F32)],
        compiler_params=_params("parallel"),
        name="conformer_sample",
    )(glu, state, w, b, g, bb)


def _attn_sample_kernel(n_pages, per_step, pt_ref, q_ref, bias_ref, tri_ref, later_ref, *refs):
    o_ref = refs[2 * per_step * n_pages]
    own = (lax.broadcasted_iota(jnp.int32, (H_B, D_B), 1) // DH_B
           == lax.broadcasted_iota(jnp.int32, (H_B, D_B), 0))
    for r in range(per_step):
        k_refs = refs[r * n_pages:(r + 1) * n_pages]
        v_refs = refs[(per_step + r) * n_pages:(per_step + r + 1) * n_pages]
        q_rows = jnp.broadcast_to(q_ref[r].astype(F32), (H_B, D_B))
        qmat = jnp.where(own, q_rows, 0.0).astype(BF16)
        s = jnp.concatenate([_dot(qmat, k_refs[pg][0, 0].astype(BF16)) for pg in range(n_pages)], axis=0)
        s = s + bias_ref[...]
        cs = _split_dot(_softplus2(s), tri_ref[...])
        page_tot = jnp.broadcast_to(cs[:, 0:1], cs.shape)
        cs = cs + _collect(later_ref[...], page_tot)
        w = jnp.exp2(s - cs).astype(BF16)
        acc = jnp.zeros((H_B, D_B), F32)
        for pg in range(n_pages):
            acc = acc + _dot_nt(w[pg * H_B:(pg + 1) * H_B, :], v_refs[pg][0, 0].astype(BF16))
        o_ref[r] = jnp.sum(jnp.where(own, acc, 0.0), axis=0, keepdims=True).astype(BF16)


def _attn_sample(page_table, q, bias, cache_kt, cache_vt, layer, per_step):
    n_seq, n_pages = page_table.shape
    rows = n_pages * H_B
    r = jnp.arange(rows)
    later = ((r[:, None] % H_B == r[None, :] % H_B) & (r[None, :] // H_B > r[:, None] // H_B)).astype(BF16)
    bias_rows = jnp.broadcast_to(jnp.tile(bias, n_pages)[:, None], (rows, PAGE))
    page_specs = lambda: [pl.BlockSpec((1, 1, D_B, PAGE),
                                       lambda s, pt, j=j, pg=pg: (layer, pt[(s * per_step + j) * n_pages + pg], 0, 0))
                          for j in range(per_step) for pg in range(n_pages)]
    const = lambda shape: pl.BlockSpec(shape, lambda s, pt: (0,) * len(shape))
    grid_spec = pltpu.PrefetchScalarGridSpec(
        num_scalar_prefetch=1,
        grid=(n_seq // per_step,),
        in_specs=[pl.BlockSpec((per_step, 1, D_B), lambda s, pt: (s, 0, 0)),
                  const((rows, PAGE)), const((PAGE, PAGE)), const((rows, rows))]
                 + page_specs() + page_specs(),
        out_specs=pl.BlockSpec((per_step, 1, D_B), lambda s, pt: (s, 0, 0)),
    )
    n_ops = per_step * n_pages
    out = pl.pallas_call(
        functools.partial(_attn_sample_kernel, n_pages, per_step),
        grid_spec=grid_spec,
        out_shape=jax.ShapeDtypeStruct((n_seq, 1, D_B), BF16),
        compiler_params=_params("arbitrary"),
        name="attn_sample",
    )(page_table.reshape(-1), q.reshape(n_seq, 1, D_B), bias_rows, _tri_keys_after(PAGE), later,
      *([cache_kt] * n_ops), *([cache_vt] * n_ops))
    return out.reshape(n_seq, D_B)


def _ssd_sample_prep_kernel(x_ref, st_ref, dt_ref, cw_ref, cb_ref, dtb_ref, alog_ref, e_ref,
                            nst_ref, xs_ref, xdt_t_ref, dec_t_ref, b_ref, c_ref):
    x = x_ref[...]
    hist = CONV_C_W - 1
    conv = cb_ref[...] + cw_ref[hist:hist + 1, :] * x
    for k in range(hist):
        conv = conv + cw_ref[k:k + 1, :] * st_ref[k]
    for k in range(hist - 1):
        nst_ref[k] = st_ref[k + 1]
    nst_ref[hist - 1] = x
    xbc = _silu(conv)
    xs = xbc[:, :D_INNER]
    xs_ref[...] = xs
    b_ref[...] = xbc[:, D_INNER:D_INNER + G_C * N_C]
    c_ref[...] = xbc[:, D_INNER + G_C * N_C:]
    dt = _softplus(dt_ref[...] + dtb_ref[...])
    e = e_ref[...]
    xdt_t_ref[...] = (xs * _spread(dt, e)).T
    dec_t_ref[...] = jnp.exp(_spread(dt * (-jnp.exp(alog_ref[...])), e)).T


def _ssd_sample_prep(xbc, state, dt, cw, cb, dtb, alog, e):
    n = xbc.shape[0]
    shapes = [(state.shape, F32), ((n, D_INNER), F32), ((D_INNER, n), F32), ((D_INNER, n), F32),
              ((n, G_C * N_C), F32), ((n, G_C * N_C), F32)]
    full = lambda s: pl.BlockSpec(s, lambda i: (0,) * len(s))
    return pl.pallas_call(
        _ssd_sample_prep_kernel,
        grid=(1,),
        in_specs=[full(xbc.shape), full(state.shape), full(dt.shape), full(cw.shape), full(cb.shape),
                  full(dtb.shape), full(alog.shape), full(e.shape)],
        out_specs=[full(s) for s, _ in shapes],
        out_shape=[jax.ShapeDtypeStruct(s, d) for s, d in shapes],
        compiler_params=_params("arbitrary"),
        name="ssd_sample_prep",
    )(xbc, state, dt, cw, cb, dtb, alog, e)


def _ssd_sample_state_kernel(xdt_t_ref, dec_t_ref, b_ref, c_ref, h_ref, *rest):
    nh_ref, y_t_ref = rest[-2:]
    n = xdt_t_ref.shape[1]
    per_step = h_ref.shape[0]
    rows = D_INNER // G_C

    @pl.when(pl.program_id(0) == 0)
    def _():
        y_t_ref[...] = jnp.zeros_like(y_t_ref)

    for jj in range(per_step):
        j = pl.program_id(0) * per_step + jj
        mine = lax.broadcasted_iota(jnp.int32, (rows, n), 1) == j
        for g in range(G_C):
            r0 = g * rows
            pick = lambda ref: jnp.sum(jnp.where(mine, ref[r0:r0 + rows, :], 0.0), axis=1, keepdims=True)
            h_new = (pick(dec_t_ref) * h_ref[jj, r0:r0 + rows, :]
                     + pick(xdt_t_ref) * b_ref[jj, :, g * N_C:(g + 1) * N_C])
            nh_ref[jj, r0:r0 + rows, :] = h_new
            y = jnp.sum(h_new * c_ref[jj, :, g * N_C:(g + 1) * N_C], axis=1, keepdims=True)
            y_t_ref[r0:r0 + rows, :] = jnp.where(mine, y, y_t_ref[r0:r0 + rows, :])


def _ssd_sample_state(xdt_t, dec_t, bm, cm, state, layer, prev_state, per_step):
    depth, n = state.shape[:2]
    full = lambda s: pl.BlockSpec(s, lambda j: (0,) * len(s))
    per_seq = pl.BlockSpec((per_step, 1, G_C * N_C), lambda j: (j, 0, 0))
    slab = pl.BlockSpec((None, per_step, D_INNER, N_C), lambda j: (layer, j, 0, 0))
    prev, prev_specs, aliases = _alias_prev(prev_state, 5, 0)
    return pl.pallas_call(
        _ssd_sample_state_kernel,
        grid=(n // per_step,),
        in_specs=[full(xdt_t.shape), full(dec_t.shape), per_seq, per_seq, slab] + prev_specs,
        out_specs=[slab, full((D_INNER, n))],
        out_shape=[jax.ShapeDtypeStruct(state.shape, F32), jax.ShapeDtypeStruct((D_INNER, n), F32)],
        input_output_aliases=aliases,
        compiler_params=_params("arbitrary"),
        name="ssd_sample_state",
    )(xdt_t, dec_t, bm.reshape(n, 1, -1), cm.reshape(n, 1, -1), state, *prev)


def _ssd_sample_out_kernel(y_t_ref, xs_ref, z_ref, dskip_ref, g_ref, c_ref):
    c_ref[...] = _gated_group_norm(y_t_ref[...].T, xs_ref[...], z_ref[...], dskip_ref[...],
                                   g_ref[...]).astype(BF16)


def _ssd_sample_out(y_t, xs, z, dskip, g):
    n = xs.shape[0]
    full = lambda s: pl.BlockSpec(s, lambda i: (0,) * len(s))
    return pl.pallas_call(
        _ssd_sample_out_kernel,
        grid=(1,),
        in_specs=[full(y_t.shape), full(xs.shape), full(z.shape), full(dskip.shape), full(g.shape)],
        out_specs=full((n, D_INNER)),
        out_shape=jax.ShapeDtypeStruct((n, D_INNER), BF16),
        compiler_params=_params("arbitrary"),
        name="ssd_sample_out",
    )(y_t, xs, z, dskip, g)


def _token_tile(t):
    for tile in (512, 256):
        if t % tile == 0:
            return tile
    return t


def _tri_keys_after(n):
    i = jnp.arange(n)
    return (i[:, None] >= i[None, :]).astype(BF16)


def kernel(x_prompt, x_sample, cache_k, cache_v, state_conv_a, state_conv_ssm, state_ssm, page_table,
           p_prompt, p_sample, g_mix, w_in, conv_a_w, conv_a_b, ln_a_g, ln_a_b, g_q, g_k, sb_bias,
           conv_ssm_w, conv_ssm_b, dt_bias, a_log, d_skip, g_ssm, w_out, g_ffn, w_ffn_gate, w_ffn_up,
           w_ffn_down, g_ple, w_ple_gate, w_ple):
    depth = w_in.shape[0]
    batch, seq, _ = x_prompt.shape
    n_seq = x_sample.shape[0]
    tp = batch * seq
    assert x_sample.shape[1] == 1
    assert seq % CONF_TILE == 0 and seq % ATT_TILE == 0 and seq % SSD_CHUNK == 0

    pad_lanes = lambda v: jnp.pad(v, ((0, 0), (0, LANES - v.shape[1])))
    seg = (jnp.arange(D_B)[:, None] // DH_B == jnp.arange(D_B)[None, :] // DH_B).astype(BF16)
    head_lanes = (jnp.arange(LANES)[:, None] == jnp.arange(D_INNER)[None, :] // P_C).astype(BF16)
    tril = (jnp.arange(SSD_CHUNK)[:, None] >= jnp.arange(SSD_CHUNK)[None, :]).astype(BF16)
    tri_att = _tri_keys_after(ATT_TILE)

    n_pool = cache_k.shape[1]
    cache_kt = jnp.transpose(cache_k, (0, 1, 3, 4, 2)).reshape(depth, n_pool, D_B, PAGE)
    cache_vt = jnp.transpose(cache_v, (0, 1, 3, 4, 2)).reshape(depth, n_pool, D_B, PAGE)
    conv_a_taps = jnp.transpose(state_conv_a, (0, 2, 1, 3))
    conv_c_taps = jnp.transpose(state_conv_ssm, (0, 2, 1, 3))
    ssm_flat = state_ssm.reshape(depth, n_seq, D_INNER, N_C)

    h_p = x_prompt.reshape(tp, D_MODEL)
    h_s = x_sample.reshape(n_seq, D_MODEL)
    pe_p = p_prompt.reshape(depth, tp, D_PLE)
    pe_s = p_sample.reshape(depth, n_seq, D_PLE)
    w_in_b = jnp.pad(w_in, ((0, 0), (0, 0), (0, W_IN_COLS - w_in.shape[2]))).astype(BF16)
    wo, wg, wu, wd, wpg, wp = (w.astype(BF16) for w in (w_out, w_ffn_gate, w_ffn_up, w_ffn_down,
                                                        w_ple_gate, w_ple))
    tile_p, tile_s = _token_tile(seq), _token_tile(n_seq)
    seqs_per_step = 8 if n_seq % 8 == 0 else 1
    kv_p = kv_s = ssm_s = None
    outs = {name: [] for name in ("cap", "ccp", "ssp", "cas", "ccs")}
    for i in range(depth):
        g_mix_i = g_mix[i][None]
        gq = jnp.tile(g_q[i], H_B)[None]
        gk = jnp.tile(g_k[i], H_B)[None]
        conv_a_w_i = jnp.pad(conv_a_w[i], ((0, CONF_PAD - CONV_A_W), (0, 0)))
        conv_c_w_i = jnp.pad(conv_ssm_w[i], ((0, SUBLANES - CONV_C_W), (0, 0)))
        dtb = pad_lanes(dt_bias[i][None])
        alog = pad_lanes(a_log[i][None])
        dskip = jnp.repeat(d_skip[i], P_C)[None]
        bias2 = sb_bias[i] * LOG2E
        post_w = (wo, g_ffn[i][None], wg, wu, wd, g_ple[i][None], wpg, wp)

        glu, q, kt, vt, kb, vb, z, xbc, dt = _in_proj(h_p, g_mix_i, w_in_b, gq, gk, seg, tile_p,
                                                      batch, i, kv_p)
        kv_p = (kt, vt)
        a_out, a_tail = _conf_prompt(glu, conv_a_w_i, conv_a_b[i][None], ln_a_g[i][None],
                                     ln_a_b[i][None], batch)
        b_out = _attn_prompt(bias2, q, kb, vb, tri_att, batch)
        c_out, ssm_p = _ssd_prompt(xbc, z, dt, conv_c_w_i, conv_ssm_b[i][None], dtb, alog, dskip,
                                   g_ssm[i][None], tril, head_lanes, batch)
        h_p = _post(h_p, a_out, b_out, c_out, pe_p, *post_w, tile_p, i)
        outs["cap"].append(a_tail[:, CONF_PAD - (CONV_A_W - 1):])
        outs["ccp"].append(xbc.reshape(batch, seq, CONV_C_DIM)[:, seq - (CONV_C_W - 1):])
        outs["ssp"].append(ssm_p.reshape(batch, H_C, P_C, N_C))

        glu, q, kt, vt, _, _, z, xbc, dt = _in_proj(h_s, g_mix_i, w_in_b, gq, gk, seg, tile_s,
                                                    1, i, kv_s)
        kv_s = (kt, vt)
        a_out, conv_a_new = _conf_sample(glu, conv_a_taps[i], conv_a_w_i, conv_a_b[i][None],
                                         ln_a_g[i][None], ln_a_b[i][None], min(n_seq, 32))
        b_out = _attn_sample(page_table, q, bias2, cache_kt, cache_vt, i, 2 if n_seq % 2 == 0 else 1)
        conv_c_new, xs, xdt_t, dec_t, bm, cm = _ssd_sample_prep(
            xbc, conv_c_taps[i], dt, conv_c_w_i, conv_ssm_b[i][None], dtb, alog, head_lanes)
        ssm_s, y_t = _ssd_sample_state(xdt_t, dec_t, bm, cm, ssm_flat, i,
                                       None if ssm_s is None else (ssm_s,), seqs_per_step)
        c_out = _ssd_sample_out(y_t, xs, z, dskip, g_ssm[i][None])
        h_s = _post(h_s, a_out, b_out, c_out, pe_s, *post_w, tile_s, i)
        outs["cas"].append(jnp.transpose(conv_a_new, (1, 0, 2)))
        outs["ccs"].append(jnp.transpose(conv_c_new, (1, 0, 2)))

    st = lambda name: jnp.stack(outs[name])
    heads_last = lambda x: jnp.transpose(x.reshape(depth, x.shape[1], H_B, DH_B, x.shape[3]), (0, 1, 4, 2, 3))
    k_p, v_p = (heads_last(x) for x in kv_p)
    k_s, v_s = (jnp.transpose(heads_last(x), (0, 2, 1, 3, 4)) for x in kv_s)
    return (h_p.reshape(batch, seq, D_MODEL), h_s.reshape(n_seq, 1, D_MODEL),
            k_p, v_p, st("cap"), st("ccp"), st("ssp"),
            k_s, v_s, st("cas"), st("ccs"), ssm_s.reshape(depth, n_seq, H_C, P_C, N_C))
```

```python
import functools
import math

import jax
import jax.numpy as jnp
import numpy as np
from jax import lax
from jax.experimental import pallas as pl
from jax.experimental.pallas import tpu as pltpu

F32 = jnp.float32
BF16 = jnp.bfloat16

D_MODEL = 1024
C_A = 512
CONV_A_W = 31
H_B = 8
DH_B = 64
D_B = H_B * DH_B
D_INNER = 1024
H_C = 16
P_C = 64
G_C = 2
N_C = 128
CONV_C_W = 4
CONV_C_DIM = D_INNER + 2 * G_C * N_C
D_FF = 2816
D_PLE = 256
EPS = 1e-6
PAGE = 128

LANES = 128
SUBLANES = 8
VMEM_LIMIT = 56 * 1024 * 1024

OFF_A = 0
OFF_Q = 2 * C_A
OFF_K = OFF_Q + D_B
OFF_V = OFF_K + D_B
OFF_Z = OFF_V + D_B
OFF_X = OFF_Z + D_INNER
OFF_DT = OFF_X + CONV_C_DIM
W_IN_COLS = OFF_DT + LANES

FF_CHUNK = 256
SSD_CHUNK = 128
CONF_TILE = 512
CONF_ROWS = 32
CONF_PAD = 32
ATT_TILE = 256
HEADS_PER_STEP = LANES // DH_B
LOG2E = 1.4426950408889634


def _const_spec(shape):
    zeros = (0,) * len(shape)
    return pl.BlockSpec(shape, lambda *_: zeros, pipeline_mode=pl.Buffered(1))


def _layer_spec(shape, layer):
    zeros = (0,) * len(shape)
    return pl.BlockSpec((None,) + tuple(shape), lambda *_: (layer,) + zeros, pipeline_mode=pl.Buffered(1))


def _alias_prev(prev, first_in, first_out):
    if prev is None:
        return (), [], {}
    specs = [pl.BlockSpec(memory_space=pl.ANY) for _ in prev]
    return tuple(prev), specs, {first_in + j: first_out + j for j in range(len(prev))}


def _params(*sem):
    return pltpu.CompilerParams(dimension_semantics=sem, vmem_limit_bytes=VMEM_LIMIT)


def _sigmoid(x):
    return 0.5 * jnp.tanh(0.5 * x) + 0.5


def _silu(x):
    h = 0.5 * x
    return h + h * jnp.tanh(h)


def _softplus(x):
    return jnp.maximum(x, 0.0) + jnp.log(1.0 + jnp.exp(-jnp.abs(x)))


def _softplus2(x):
    sign = jnp.uint32(0x80000000)
    neg_abs = lax.bitcast_convert_type(lax.bitcast_convert_type(x, jnp.uint32) | sign, F32)
    return jnp.maximum(x, 0.0) + jnp.log(1.0 + jnp.exp2(neg_abs)) * LOG2E


def _rms(x, g):
    return x * lax.rsqrt(jnp.mean(x * x, axis=-1, keepdims=True) + EPS) * g


def _dot(a, b):
    return jnp.dot(a, b, preferred_element_type=F32)


def _dot_nt(a, b):
    return lax.dot_general(a, b, (((1,), (1,)), ((), ())), preferred_element_type=F32)


def _three_parts(x):
    hi = x.astype(BF16)
    rest = x - hi.astype(F32)
    mid = rest.astype(BF16)
    return hi, mid, (rest - mid.astype(F32)).astype(BF16)


def _spread(x, m):
    hi, mid, lo = _three_parts(x)
    return _dot(hi, m) + _dot(mid, m) + _dot(lo, m)


def _collect(m, x):
    hi, mid, lo = _three_parts(x)
    return _dot(m, hi) + _dot(m, mid) + _dot(m, lo)


def _split_dot(x, m):
    hi = x.astype(BF16)
    lo = (x - hi.astype(F32)).astype(BF16)
    return _dot(hi, m) + _dot(lo, m)


def _in_proj_kernel(h_ref, g_ref, w_ref, gq_ref, gk_ref, seg_ref, *rest):
    a_ref, q_ref, k_ref, v_ref, kb_ref, vb_ref, z_ref, x_ref, dt_ref = rest[-9:]
    u = _rms(h_ref[...], g_ref[...]).astype(BF16)

    def proj(lo, hi):
        return _dot(u, w_ref[:, lo:hi])

    a = proj(OFF_A, OFF_Q)
    a_ref[...] = a[:, :C_A] * _sigmoid(a[:, C_A:])
    z_ref[...] = proj(OFF_Z, OFF_X)
    dt_ref[...] = proj(OFF_DT, W_IN_COLS)
    x_ref[...] = proj(OFF_X, OFF_DT)

    seg = seg_ref[...]

    def head_norm(t, g):
        ms = _dot((t * t).astype(BF16), seg) * (1.0 / DH_B)
        return t * lax.rsqrt(ms + EPS) * g

    qn = head_norm(proj(OFF_Q, OFF_K), gq_ref[...])
    kn = head_norm(proj(OFF_K, OFF_V), gk_ref[...])
    v = proj(OFF_V, OFF_Z)
    q_ref[...] = (qn * (DH_B ** -0.5 * LOG2E)).astype(BF16)
    k_ref[0] = kn.T
    v_ref[0] = v.T
    kb_ref[...] = kn.astype(BF16)
    vb_ref[...] = v.astype(BF16)


def _in_proj(h, g_mix, w_in, gq, gk, seg, tm, batch, layer, prev_kv):
    t = h.shape[0]
    depth = w_in.shape[0]
    seq = t // batch
    nt = seq // tm
    row = lambda n: pl.BlockSpec((tm, n), lambda i: (i, 0))
    col = pl.BlockSpec((None, 1, D_B, tm), lambda i: (layer, i // nt, 0, i % nt))
    rows = lambda n, d: (row(n), jax.ShapeDtypeStruct((t, n), d))
    cols = (col, jax.ShapeDtypeStruct((depth, batch, D_B, seq), F32))
    outs = [rows(C_A, F32), rows(D_B, BF16), cols, cols, rows(D_B, BF16), rows(D_B, BF16),
            rows(D_INNER, F32), rows(CONV_C_DIM, F32), rows(LANES, F32)]
    prev, prev_specs, aliases = _alias_prev(prev_kv, 6, 2)
    return pl.pallas_call(
        _in_proj_kernel,
        grid=(t // tm,),
        in_specs=[row(D_MODEL), _const_spec((1, D_MODEL)), _layer_spec((D_MODEL, W_IN_COLS), layer),
                  _const_spec((1, D_B)), _const_spec((1, D_B)), _const_spec((D_B, D_B))] + prev_specs,
        out_specs=[spec for spec, _ in outs],
        out_shape=[shape for _, shape in outs],
        input_output_aliases=aliases,
        compiler_params=_params("parallel"),
        name="in_proj",
    )(h, g_mix, w_in, gq, gk, seg, *prev)


def _layer_norm_silu(x, g, b):
    mu = jnp.mean(x, axis=-1, keepdims=True)
    xc = x - mu
    y = xc * lax.rsqrt(jnp.mean(xc * xc, axis=-1, keepdims=True) + EPS)
    return _silu(y * g + b)


def _conf_prompt_kernel(a_ref, w_ref, b_ref, g_ref, bb_ref, o_ref, tail_ref, buf, shifted):
    tile = a_ref.shape[0]

    @pl.when(pl.program_id(1) == 0)
    def _():
        buf[0:CONF_PAD, :] = jnp.zeros((CONF_PAD, C_A), F32)

    buf[CONF_PAD:CONF_PAD + tile, :] = a_ref[...]
    n_shift = shifted.shape[1]
    for r in range(1, SUBLANES):
        shifted[r - 1] = buf[r:r + n_shift, :]
    first = CONF_PAD - (CONV_A_W - 1)
    for r0 in range(0, tile, CONF_ROWS):
        acc = jnp.zeros((CONF_ROWS, C_A), F32)
        for k in range(CONV_A_W):
            r = (first + k) % SUBLANES
            base = first + r0 + k - r
            rows = buf[base:base + CONF_ROWS, :] if r == 0 else shifted[r - 1, base:base + CONF_ROWS, :]
            acc = acc + w_ref[k:k + 1, :] * rows
        y = _layer_norm_silu(acc + b_ref[...], g_ref[...], bb_ref[...])
        o_ref[r0:r0 + CONF_ROWS, :] = y.astype(BF16)
    tail = buf[tile:tile + CONF_PAD, :]
    buf[0:CONF_PAD, :] = tail
    tail_ref[0] = tail


def _conf_prompt(glu, w, b, g, bb, batch):
    t = glu.shape[0]
    nt = t // batch // CONF_TILE
    return pl.pallas_call(
        _conf_prompt_kernel,
        grid=(batch, nt),
        in_specs=[pl.BlockSpec((CONF_TILE, C_A), lambda bi, ti: (bi * nt + ti, 0)),
                  _const_spec((CONF_PAD, C_A)), _const_spec((1, C_A)), _const_spec((1, C_A)),
                  _const_spec((1, C_A))],
        out_specs=[pl.BlockSpec((CONF_TILE, C_A), lambda bi, ti: (bi * nt + ti, 0)),
                   pl.BlockSpec((1, CONF_PAD, C_A), lambda bi, ti: (bi, 0, 0))],
        out_shape=[jax.ShapeDtypeStruct((t, C_A), BF16),
                   jax.ShapeDtypeStruct((batch, CONF_PAD, C_A), F32)],
        scratch_shapes=[pltpu.VMEM((CONF_PAD + CONF_TILE, C_A), F32),
                        pltpu.VMEM((SUBLANES - 1, CONF_PAD + CONF_TILE - SUBLANES, C_A), F32)],
        compiler_params=_params("parallel", "arbitrary"),
        name="conformer_prompt",
    )(glu, w, b, g, bb)


def _attn_prompt_kernel(bias_ref, q_ref, k_ref, v_ref, tri_ref, o_ref, s_ref, cs_ref, acc_ref, carry_ref):
    qi = pl.program_id(1)
    tile = q_ref.shape[0]
    tri = tri_ref[...]
    lane = lax.broadcasted_iota(jnp.int32, (tile, LANES), 1)
    row = lax.broadcasted_iota(jnp.int32, (tile, tile), 0)
    col = lax.broadcasted_iota(jnp.int32, (tile, tile), 1)
    causal = col < row

    heads = []
    for lt in range(D_B // LANES):
        q2 = q_ref[:, lt * LANES:(lt + 1) * LANES].astype(F32)
        for hh in range(HEADS_PER_STEP):
            qh = jnp.where((lane // DH_B) == hh, q2, 0.0).astype(BF16)
            heads.append((lt, qh, bias_ref[lt * HEADS_PER_STEP + hh]))
    n = len(heads)

    def scores(i, start):
        lt, qh, bias = heads[i]
        s_ref[i] = _dot_nt(qh, k_ref[pl.ds(start, tile), lt * LANES:(lt + 1) * LANES]) + bias

    def later_sums(i, masked):
        sp = _softplus2(s_ref[i])
        if masked:
            sp = jnp.where(causal, sp, 0.0)
        cs_ref[i] = _dot(sp.astype(BF16), tri)

    def weigh(i, start, masked):
        lt = heads[i][0]
        cs = cs_ref[i]
        carry = carry_ref[i]
        w = jnp.exp2(s_ref[i] - cs - carry)
        if masked:
            w = jnp.where(causal, w, 0.0)
        acc_ref[i] += _dot(w.astype(BF16), v_ref[pl.ds(start, tile), lt * LANES:(lt + 1) * LANES])
        carry_ref[i] = carry + cs[:, 0:1]

    acc_ref[...] = jnp.zeros_like(acc_ref)
    carry_ref[...] = jnp.zeros_like(carry_ref)
    start0 = pl.multiple_of(qi * tile, tile)
    for i in range(n):
        scores(i, start0)
    for i in range(n):
        later_sums(i, True)
    nxt0 = pl.multiple_of(jnp.maximum(qi - 1, 0) * tile, tile)
    for i in range(n):
        weigh(i, start0, True)
        scores(i, nxt0)
    for i in range(n):
        later_sums(i, False)

    @pl.loop(0, qi - 1)
    def _(j):
        cur = qi - 1 - j
        for i in range(n):
            weigh(i, pl.multiple_of(cur * tile, tile), False)
            scores(i, pl.multiple_of((cur - 1) * tile, tile))
        for i in range(n):
            later_sums(i, False)

    @pl.when(qi > 0)
    def _():
        for i in range(n):
            weigh(i, 0, False)

    for lt in range(D_B // LANES):
        out = acc_ref[lt * HEADS_PER_STEP]
        for hh in range(1, HEADS_PER_STEP):
            out = jnp.where((lane // DH_B) == hh, acc_ref[lt * HEADS_PER_STEP + hh], out)
        o_ref[:, lt * LANES:(lt + 1) * LANES] = out.astype(BF16)


def _attn_prompt(bias, q, k, v, tri, batch):
    t = q.shape[0]
    seq = t // batch
    nq = seq // ATT_TILE
    resident = lambda: pl.BlockSpec((seq, D_B), lambda b, i, *_: (b, 0), pipeline_mode=pl.Buffered(1))
    grid_spec = pltpu.PrefetchScalarGridSpec(
        num_scalar_prefetch=1,
        grid=(batch, nq),
        in_specs=[pl.BlockSpec((ATT_TILE, D_B), lambda b, i, *_: (b * nq + i, 0)),
                  resident(), resident(),
                  pl.BlockSpec((ATT_TILE, ATT_TILE), lambda b, i, *_: (0, 0))],
        out_specs=pl.BlockSpec((ATT_TILE, D_B), lambda b, i, *_: (b * nq + i, 0)),
        scratch_shapes=[pltpu.VMEM((H_B, ATT_TILE, ATT_TILE), F32),
                        pltpu.VMEM((H_B, ATT_TILE, ATT_TILE), F32),
                        pltpu.VMEM((H_B, ATT_TILE, LANES), F32),
                        pltpu.VMEM((H_B, ATT_TILE, 1), F32)],
    )
    return pl.pallas_call(
        _attn_prompt_kernel,
        grid_spec=grid_spec,
        out_shape=jax.ShapeDtypeStruct((t, D_B), BF16),
        compiler_params=_params("parallel", "arbitrary"),
        name="attn_prompt",
    )(bias, q, k, v, tri)


def _gated_group_norm(y, xs, z, dskip, g):
    y = (y + dskip * xs) * _silu(z)
    gw = D_INNER // G_C
    return jnp.concatenate(
        [_rms(y[:, i * gw:(i + 1) * gw], g[:, i * gw:(i + 1) * gw]) for i in range(G_C)], axis=1)


def _ssd_prompt_kernel(x_ref, z_ref, dt_ref, cw_ref, cb_ref, dtb_ref, alog_ref, dskip_ref, g_ref,
                       tril_ref, e_ref, c_ref, st_ref, cbuf, ybuf):
    q = x_ref.shape[0]

    @pl.when(pl.program_id(1) == 0)
    def _():
        cbuf[0:SUBLANES, :] = jnp.zeros((SUBLANES, CONV_C_DIM), F32)
        st_ref[...] = jnp.zeros_like(st_ref)

    cbuf[SUBLANES:SUBLANES + q, :] = x_ref[...]
    first = SUBLANES - (CONV_C_W - 1)
    conv = cb_ref[...] + sum(cw_ref[k:k + 1, :] * cbuf[first + k:first + k + q, :]
                             for k in range(CONV_C_W))
    cbuf[0:SUBLANES, :] = cbuf[q:q + SUBLANES, :]
    xbc = _silu(conv)
    xs = xbc[:, :D_INNER]
    bm = xbc[:, D_INNER:D_INNER + G_C * N_C].astype(BF16)
    cm = xbc[:, D_INNER + G_C * N_C:].astype(BF16)

    dt = _softplus(dt_ref[...] + dtb_ref[...])
    a_cum = _collect(tril_ref[...], dt * (-jnp.exp(alog_ref[...])))
    e = e_ref[...]
    dt_x = _spread(dt, e)
    a_x = _spread(a_cum, e)
    a_cum_t = a_cum.T
    a_xt = a_x.T
    xdt = xs * dt_x
    exp_a_x = jnp.exp(a_x)

    row = lax.broadcasted_iota(jnp.int32, (q, q), 0)
    col = lax.broadcasted_iota(jnp.int32, (q, q), 1)
    causal = col <= row
    lane = lax.broadcasted_iota(jnp.int32, (q, LANES), 1)
    heads_per_group = H_C // G_C

    for g in range(G_C):
        bg = bm[:, g * N_C:(g + 1) * N_C]
        cg = cm[:, g * N_C:(g + 1) * N_C]
        cb = _dot_nt(cg, bg)
        for pp in range(heads_per_group // HEADS_PER_STEP):
            pair = g * (heads_per_group // HEADS_PER_STEP) + pp
            lo = pair * LANES
            xdt_p = xdt[:, lo:lo + LANES]
            xdt_pb = xdt_p.astype(BF16)
            y_p = None
            for hh in range(HEADS_PER_STEP):
                h = pair * HEADS_PER_STEP + hh
                seg = a_cum[:, h:h + 1] - a_cum_t[h:h + 1, :]
                m = (cb * jnp.exp(jnp.where(causal, seg, -jnp.inf))).astype(BF16)
                r = _dot(m, xdt_pb)
                y_p = r if y_p is None else jnp.where((lane // P_C) == hh, r, y_p)
            h_prev = st_ref[0, lo:lo + LANES, :]
            y_p = y_p + _dot_nt(cg, h_prev.astype(BF16)) * exp_a_x[:, lo:lo + LANES]
            ybuf[:, lo:lo + LANES] = y_p
            a_t = a_xt[lo:lo + LANES, :]
            tot = a_t[:, q - 1:q]
            s_new = _dot((xdt_p.T * jnp.exp(tot - a_t)).astype(BF16), bg)
            st_ref[0, lo:lo + LANES, :] = jnp.exp(tot) * h_prev + s_new

    c_ref[...] = _gated_group_norm(ybuf[...], xs, z_ref[...], dskip_ref[...], g_ref[...]).astype(BF16)


def _ssd_prompt(xbc, z, dt, cw, cb, dtb, alog, dskip, g, tril, e, batch):
    t = xbc.shape[0]
    nc = t // batch // SSD_CHUNK
    row = lambda n: pl.BlockSpec((SSD_CHUNK, n), lambda bi, ci: (bi * nc + ci, 0))
    return pl.pallas_call(
        _ssd_prompt_kernel,
        grid=(batch, nc),
        in_specs=[row(CONV_C_DIM), row(D_INNER), row(LANES),
                  _const_spec((SUBLANES, CONV_C_DIM)), _const_spec((1, CONV_C_DIM)),
                  _const_spec((1, LANES)), _const_spec((1, LANES)), _const_spec((1, D_INNER)),
                  _const_spec((1, D_INNER)), _const_spec((SSD_CHUNK, SSD_CHUNK)),
                  _const_spec((LANES, D_INNER))],
        out_specs=[row(D_INNER), pl.BlockSpec((1, D_INNER, N_C), lambda bi, ci: (bi, 0, 0))],
        out_shape=[jax.ShapeDtypeStruct((t, D_INNER), BF16),
                   jax.ShapeDtypeStruct((batch, D_INNER, N_C), F32)],
        scratch_shapes=[pltpu.VMEM((SUBLANES + SSD_CHUNK, CONV_C_DIM), F32),
                        pltpu.VMEM((SSD_CHUNK, D_INNER), F32)],
        compiler_params=_params("parallel", "arbitrary"),
        name="ssd_prompt",
    )(xbc, z, dt, cw, cb, dtb, alog, dskip, g, tril, e)


def _post_kernel(h_ref, a_ref, b_ref, c_ref, pe_ref, wo_ref, gf_ref, wg_ref, wu_ref, wd_ref,
                 gp_ref, wpg_ref, wp_ref, o_ref):
    h = (h_ref[...] + _dot(a_ref[...], wo_ref[0:C_A, :]) + _dot(b_ref[...], wo_ref[C_A:C_A + D_B, :])
         + _dot(c_ref[...], wo_ref[C_A + D_B:, :]))
    f = _rms(h, gf_ref[...]).astype(BF16)
    acc = jnp.zeros_like(h)
    for c0 in range(0, D_FF, FF_CHUNK):
        gate = _dot(f, wg_ref[:, c0:c0 + FF_CHUNK])
        up = _dot(f, wu_ref[:, c0:c0 + FF_CHUNK])
        acc = acc + _dot((_silu(gate) * up).astype(BF16), wd_ref[c0:c0 + FF_CHUNK, :])
    h = h + acc
    gate = _sigmoid(_dot(_rms(h, gp_ref[...]).astype(BF16), wpg_ref[...]))
    o_ref[...] = h + _dot(pe_ref[...].astype(BF16), wp_ref[...]) * gate


def _post(h, a, b, c, pe, wo, gf, wg, wu, wd, gp, wpg, wp, tm, layer):
    t = h.shape[0]
    row = lambda n: pl.BlockSpec((tm, n), lambda i: (i, 0))
    weight = lambda w: _layer_spec(w.shape[1:], layer)
    return pl.pallas_call(
        _post_kernel,
        grid=(t // tm,),
        in_specs=[row(D_MODEL), row(C_A), row(D_B), row(D_INNER),
                  pl.BlockSpec((None, tm, D_PLE), lambda i: (layer, i, 0)),
                  weight(wo), _const_spec((1, D_MODEL)), weight(wg), weight(wu), weight(wd),
                  _const_spec((1, D_MODEL)), weight(wpg), weight(wp)],
        out_specs=row(D_MODEL),
        out_shape=jax.ShapeDtypeStruct((t, D_MODEL), F32),
        compiler_params=_params("parallel"),
        name="post",
    )(h, a, b, c, pe, wo, gf, wg, wu, wd, gp, wpg, wp)


def _conf_sample_kernel(a_ref, st_ref, w_ref, b_ref, g_ref, bb_ref, o_ref, nst_ref):
    glu = a_ref[...]
    hist = CONV_A_W - 1
    acc = b_ref[...] + w_ref[hist:hist + 1, :] * glu
    for k in range(hist):
        acc = acc + w_ref[k:k + 1, :] * st_ref[k]
    o_ref[...] = _layer_norm_silu(acc, g_ref[...], bb_ref[...]).astype(BF16)
    for k in range(hist - 1):
        nst_ref[k] = st_ref[k + 1]
    nst_ref[hist - 1] = glu


def _conf_sample(glu, state, w, b, g, bb, nb):
    n = glu.shape[0]
    hist = CONV_A_W - 1
    row = lambda m: pl.BlockSpec((nb, m), lambda i: (i, 0))
    taps = pl.BlockSpec((hist, nb, C_A), lambda i: (0, i, 0))
    return pl.pallas_call(
        _conf_sample_kernel,
        grid=(n // nb,),
        in_specs=[row(C_A), taps, _const_spec((CONF_PAD, C_A)), _const_spec((1, C_A)),
                  _const_spec((1, C_A)), _const_spec((1, C_A))],
        out_specs=[row(C_A), taps],
        out_shape=[jax.ShapeDtypeStruct((n, C_A), BF16), jax.ShapeDtypeStruct((hist, n, C_A), F32)],
        compiler_params=_params("parallel"),
        name="conformer_sample",
    )(glu, state, w, b, g, bb)


def _attn_sample_kernel(n_pages, per_step, pt_ref, q_ref, bias_ref, tri_ref, later_ref, *refs):
    o_ref = refs[2 * per_step * n_pages]
    own = (lax.broadcasted_iota(jnp.int32, (H_B, D_B), 1) // DH_B
           == lax.broadcasted_iota(jnp.int32, (H_B, D_B), 0))
    for r in range(per_step):
        k_refs = refs[r * n_pages:(r + 1) * n_pages]
        v_refs = refs[(per_step + r) * n_pages:(per_step + r + 1) * n_pages]
        q_rows = jnp.broadcast_to(q_ref[r].astype(F32), (H_B, D_B))
        qmat = jnp.where(own, q_rows, 0.0).astype(BF16)
        s = jnp.concatenate([_dot(qmat, k_refs[pg][0, 0].astype(BF16)) for pg in range(n_pages)], axis=0)
        s = s + bias_ref[...]
        cs = _split_dot(_softplus2(s), tri_ref[...])
        page_tot = jnp.broadcast_to(cs[:, 0:1], cs.shape)
        cs = cs + _collect(later_ref[...], page_tot)
        w = jnp.exp2(s - cs).astype(BF16)
        acc = jnp.zeros((H_B, D_B), F32)
        for pg in range(n_pages):
            acc = acc + _dot_nt(w[pg * H_B:(pg + 1) * H_B, :], v_refs[pg][0, 0].astype(BF16))
        o_ref[r] = jnp.sum(jnp.where(own, acc, 0.0), axis=0, keepdims=True).astype(BF16)


def _attn_sample(page_table, q, bias, cache_kt, cache_vt, layer, per_step):
    n_seq, n_pages = page_table.shape
    rows = n_pages * H_B
    r = np.arange(rows)
    later = _const01((r[:, None] % H_B == r[None, :] % H_B) & (r[None, :] // H_B > r[:, None] // H_B))
    bias_rows = jnp.broadcast_to(jnp.tile(bias, n_pages)[:, None], (rows, PAGE))
    page_specs = lambda: [pl.BlockSpec((1, 1, D_B, PAGE),
                                       lambda s, pt, j=j, pg=pg: (layer, pt[(s * per_step + j) * n_pages + pg], 0, 0))
                          for j in range(per_step) for pg in range(n_pages)]
    const = lambda shape: pl.BlockSpec(shape, lambda s, pt: (0,) * len(shape))
    grid_spec = pltpu.PrefetchScalarGridSpec(
        num_scalar_prefetch=1,
        grid=(n_seq // per_step,),
        in_specs=[pl.BlockSpec((per_step, 1, D_B), lambda s, pt: (s, 0, 0)),
                  const((rows, PAGE)), const((PAGE, PAGE)), const((rows, rows))]
                 + page_specs() + page_specs(),
        out_specs=pl.BlockSpec((per_step, 1, D_B), lambda s, pt: (s, 0, 0)),
    )
    n_ops = per_step * n_pages
    out = pl.pallas_call(
        functools.partial(_attn_sample_kernel, n_pages, per_step),
        grid_spec=grid_spec,
        out_shape=jax.ShapeDtypeStruct((n_seq, 1, D_B), BF16),
        compiler_params=_params("arbitrary"),
        name="attn_sample",
    )(page_table.reshape(-1), q.reshape(n_seq, 1, D_B), bias_rows, _tri_keys_after(PAGE), later,
      *([cache_kt] * n_ops), *([cache_vt] * n_ops))
    return out.reshape(n_seq, D_B)


def _ssd_sample_prep_kernel(x_ref, st_ref, dt_ref, cw_ref, cb_ref, dtb_ref, alog_ref, e_ref,
                            nst_ref, xs_ref, xdt_t_ref, dec_t_ref, b_ref, c_ref):
    x = x_ref[...]
    hist = CONV_C_W - 1
    conv = cb_ref[...] + cw_ref[hist:hist + 1, :] * x
    for k in range(hist):
        conv = conv + cw_ref[k:k + 1, :] * st_ref[k]
    for k in range(hist - 1):
        nst_ref[k] = st_ref[k + 1]
    nst_ref[hist - 1] = x
    xbc = _silu(conv)
    xs = xbc[:, :D_INNER]
    xs_ref[...] = xs
    b_ref[...] = xbc[:, D_INNER:D_INNER + G_C * N_C]
    c_ref[...] = xbc[:, D_INNER + G_C * N_C:]
    dt = _softplus(dt_ref[...] + dtb_ref[...])
    e = e_ref[...]
    xdt_t_ref[...] = (xs * _spread(dt, e)).T
    dec_t_ref[...] = jnp.exp(_spread(dt * (-jnp.exp(alog_ref[...])), e)).T


def _ssd_sample_prep(xbc, state, dt, cw, cb, dtb, alog, e):
    n = xbc.shape[0]
    shapes = [(state.shape, F32), ((n, D_INNER), F32), ((D_INNER, n), F32), ((D_INNER, n), F32),
              ((n, G_C * N_C), F32), ((n, G_C * N_C), F32)]
    full = lambda s: pl.BlockSpec(s, lambda i: (0,) * len(s))
    return pl.pallas_call(
        _ssd_sample_prep_kernel,
        grid=(1,),
        in_specs=[full(xbc.shape), full(state.shape), full(dt.shape), full(cw.shape), full(cb.shape),
                  full(dtb.shape), full(alog.shape), full(e.shape)],
        out_specs=[full(s) for s, _ in shapes],
        out_shape=[jax.ShapeDtypeStruct(s, d) for s, d in shapes],
        compiler_params=_params("arbitrary"),
        name="ssd_sample_prep",
    )(xbc, state, dt, cw, cb, dtb, alog, e)


def _ssd_sample_state_kernel(xdt_t_ref, dec_t_ref, b_ref, c_ref, h_ref, *rest):
    nh_ref, y_t_ref = rest[-2:]
    n = xdt_t_ref.shape[1]
    per_step = h_ref.shape[0]
    rows = D_INNER // G_C

    @pl.when(pl.program_id(0) == 0)
    def _():
        y_t_ref[...] = jnp.zeros_like(y_t_ref)

    for jj in range(per_step):
        j = pl.program_id(0) * per_step + jj
        mine = lax.broadcasted_iota(jnp.int32, (rows, n), 1) == j
        for g in range(G_C):
            r0 = g * rows
            pick = lambda ref: jnp.sum(jnp.where(mine, ref[r0:r0 + rows, :], 0.0), axis=1, keepdims=True)
            h_new = (pick(dec_t_ref) * h_ref[jj, r0:r0 + rows, :]
                     + pick(xdt_t_ref) * b_ref[jj, :, g * N_C:(g + 1) * N_C])
            nh_ref[jj, r0:r0 + rows, :] = h_new
            y = jnp.sum(h_new * c_ref[jj, :, g * N_C:(g + 1) * N_C], axis=1, keepdims=True)
            y_t_ref[r0:r0 + rows, :] = jnp.where(mine, y, y_t_ref[r0:r0 + rows, :])


def _ssd_sample_state(xdt_t, dec_t, bm, cm, state, layer, prev_state, per_step):
    depth, n = state.shape[:2]
    full = lambda s: pl.BlockSpec(s, lambda j: (0,) * len(s))
    per_seq = pl.BlockSpec((per_step, 1, G_C * N_C), lambda j: (j, 0, 0))
    slab = pl.BlockSpec((None, per_step, D_INNER, N_C), lambda j: (layer, j, 0, 0))
    prev, prev_specs, aliases = _alias_prev(prev_state, 5, 0)
    return pl.pallas_call(
        _ssd_sample_state_kernel,
        grid=(n // per_step,),
        in_specs=[full(xdt_t.shape), full(dec_t.shape), per_seq, per_seq, slab] + prev_specs,
        out_specs=[slab, full((D_INNER, n))],
        out_shape=[jax.ShapeDtypeStruct(state.shape, F32), jax.ShapeDtypeStruct((D_INNER, n), F32)],
        input_output_aliases=aliases,
        compiler_params=_params("arbitrary"),
        name="ssd_sample_state",
    )(xdt_t, dec_t, bm.reshape(n, 1, -1), cm.reshape(n, 1, -1), state, *prev)


def _ssd_sample_out_kernel(y_t_ref, xs_ref, z_ref, dskip_ref, g_ref, c_ref):
    c_ref[...] = _gated_group_norm(y_t_ref[...].T, xs_ref[...], z_ref[...], dskip_ref[...],
                                   g_ref[...]).astype(BF16)


def _ssd_sample_out(y_t, xs, z, dskip, g):
    n = xs.shape[0]
    full = lambda s: pl.BlockSpec(s, lambda i: (0,) * len(s))
    return pl.pallas_call(
        _ssd_sample_out_kernel,
        grid=(1,),
        in_specs=[full(y_t.shape), full(xs.shape), full(z.shape), full(dskip.shape), full(g.shape)],
        out_specs=full((n, D_INNER)),
        out_shape=jax.ShapeDtypeStruct((n, D_INNER), BF16),
        compiler_params=_params("arbitrary"),
        name="ssd_sample_out",
    )(y_t, xs, z, dskip, g)


def _token_tile(t):
    for tile in (512, 256):
        if t % tile == 0:
            return tile
    return t


def _const01(mask):
    return jnp.asarray(np.asarray(mask, np.float32), BF16)


def _tri_keys_after(n):
    i = np.arange(n)
    return _const01(i[:, None] >= i[None, :])


def kernel(x_prompt, x_sample, cache_k, cache_v, state_conv_a, state_conv_ssm, state_ssm, page_table,
           p_prompt, p_sample, g_mix, w_in, conv_a_w, conv_a_b, ln_a_g, ln_a_b, g_q, g_k, sb_bias,
           conv_ssm_w, conv_ssm_b, dt_bias, a_log, d_skip, g_ssm, w_out, g_ffn, w_ffn_gate, w_ffn_up,
           w_ffn_down, g_ple, w_ple_gate, w_ple):
    depth = w_in.shape[0]
    batch, seq, _ = x_prompt.shape
    n_seq = x_sample.shape[0]
    tp = batch * seq
    assert x_sample.shape[1] == 1
    assert seq % CONF_TILE == 0 and seq % ATT_TILE == 0 and seq % SSD_CHUNK == 0

    pad_lanes = lambda v: jnp.pad(v, ((0, 0), (0, LANES - v.shape[1])))
    seg = _const01(np.arange(D_B)[:, None] // DH_B == np.arange(D_B)[None, :] // DH_B)
    head_lanes = _const01(np.arange(LANES)[:, None] == np.arange(D_INNER)[None, :] // P_C)
    tril = _const01(np.arange(SSD_CHUNK)[:, None] >= np.arange(SSD_CHUNK)[None, :])
    tri_att = _tri_keys_after(ATT_TILE)

    n_pool = cache_k.shape[1]
    cache_kt = jnp.transpose(cache_k, (0, 1, 3, 4, 2)).reshape(depth, n_pool, D_B, PAGE)
    cache_vt = jnp.transpose(cache_v, (0, 1, 3, 4, 2)).reshape(depth, n_pool, D_B, PAGE)
    conv_a_taps = jnp.transpose(state_conv_a, (0, 2, 1, 3))
    conv_c_taps = jnp.transpose(state_conv_ssm, (0, 2, 1, 3))
    ssm_flat = state_ssm.reshape(depth, n_seq, D_INNER, N_C)

    h_p = x_prompt.reshape(tp, D_MODEL)
    h_s = x_sample.reshape(n_seq, D_MODEL)
    pe_p = p_prompt.reshape(depth, tp, D_PLE)
    pe_s = p_sample.reshape(depth, n_seq, D_PLE)
    w_in_b = jnp.pad(w_in, ((0, 0), (0, 0), (0, W_IN_COLS - w_in.shape[2]))).astype(BF16)
    wo, wg, wu, wd, wpg, wp = (w.astype(BF16) for w in (w_out, w_ffn_gate, w_ffn_up, w_ffn_down,
                                                        w_ple_gate, w_ple))
    tile_p, tile_s = _token_tile(seq), _token_tile(n_seq)
    seqs_per_step = 8 if n_seq % 8 == 0 else 1
    kv_p = kv_s = ssm_s = None
    outs = {name: [] for name in ("cap", "ccp", "ssp", "cas", "ccs")}
    for i in range(depth):
        g_mix_i = g_mix[i][None]
        gq = jnp.tile(g_q[i], H_B)[None]
        gk = jnp.tile(g_k[i], H_B)[None]
        conv_a_w_i = jnp.pad(conv_a_w[i], ((0, CONF_PAD - CONV_A_W), (0, 0)))
        conv_c_w_i = jnp.pad(conv_ssm_w[i], ((0, SUBLANES - CONV_C_W), (0, 0)))
        dtb = pad_lanes(dt_bias[i][None])
        alog = pad_lanes(a_log[i][None])
        dskip = jnp.repeat(d_skip[i], P_C)[None]
        bias2 = sb_bias[i] * LOG2E
        post_w = (wo, g_ffn[i][None], wg, wu, wd, g_ple[i][None], wpg, wp)

        glu, q, kt, vt, kb, vb, z, xbc, dt = _in_proj(h_p, g_mix_i, w_in_b, gq, gk, seg, tile_p,
                                                      batch, i, kv_p)
        kv_p = (kt, vt)
        a_out, a_tail = _conf_prompt(glu, conv_a_w_i, conv_a_b[i][None], ln_a_g[i][None],
                                     ln_a_b[i][None], batch)
        b_out = _attn_prompt(bias2, q, kb, vb, tri_att, batch)
        c_out, ssm_p = _ssd_prompt(xbc, z, dt, conv_c_w_i, conv_ssm_b[i][None], dtb, alog, dskip,
                                   g_ssm[i][None], tril, head_lanes, batch)
        h_p = _post(h_p, a_out, b_out, c_out, pe_p, *post_w, tile_p, i)
        outs["cap"].append(a_tail[:, CONF_PAD - (CONV_A_W - 1):])
        outs["ccp"].append(xbc.reshape(batch, seq, CONV_C_DIM)[:, seq - (CONV_C_W - 1):])
        outs["ssp"].append(ssm_p.reshape(batch, H_C, P_C, N_C))

        glu, q, kt, vt, _, _, z, xbc, dt = _in_proj(h_s, g_mix_i, w_in_b, gq, gk, seg, tile_s,
                                                    1, i, kv_s)
        kv_s = (kt, vt)
        a_out, conv_a_new = _conf_sample(glu, conv_a_taps[i], conv_a_w_i, conv_a_b[i][None],
                                         ln_a_g[i][None], ln_a_b[i][None], min(n_seq, 32))
        b_out = _attn_sample(page_table, q, bias2, cache_kt, cache_vt, i, 2 if n_seq % 2 == 0 else 1)
        conv_c_new, xs, xdt_t, dec_t, bm, cm = _ssd_sample_prep(
            xbc, conv_c_taps[i], dt, conv_c_w_i, conv_ssm_b[i][None], dtb, alog, head_lanes)
        ssm_s, y_t = _ssd_sample_state(xdt_t, dec_t, bm, cm, ssm_flat, i,
                                       None if ssm_s is None else (ssm_s,), seqs_per_step)
        c_out = _ssd_sample_out(y_t, xs, z, dskip, g_ssm[i][None])
        h_s = _post(h_s, a_out, b_out, c_out, pe_s, *post_w, tile_s, i)
        outs["cas"].append(jnp.transpose(conv_a_new, (1, 0, 2)))
        outs["ccs"].append(jnp.transpose(conv_c_new, (1, 0, 2)))

    st = lambda name: jnp.stack(outs[name])
    heads_last = lambda x: jnp.transpose(x.reshape(depth, x.shape[1], H_B, DH_B, x.shape[3]), (0, 1, 4, 2, 3))
    k_p, v_p = (heads_last(x) for x in kv_p)
    k_s, v_s = (jnp.transpose(heads_last(x), (0, 2, 1, 3, 4)) for x in kv_s)
    return (h_p.reshape(batch, seq, D_MODEL), h_s.reshape(n_seq, 1, D_MODEL),
            k_p, v_p, st("cap"), st("ccp"), st("ssp"),
            k_s, v_s, st("cas"), st("ccs"), ssm_s.reshape(depth, n_seq, H_C, P_C, N_C))
```

```python
import functools
import math

import jax
import jax.numpy as jnp
import numpy as np
from jax import lax
from jax.experimental import pallas as pl
from jax.experimental.pallas import tpu as pltpu

F32 = jnp.float32
BF16 = jnp.bfloat16

D_MODEL = 1024
C_A = 512
CONV_A_W = 31
H_B = 8
DH_B = 64
D_B = H_B * DH_B
D_INNER = 1024
H_C = 16
P_C = 64
G_C = 2
N_C = 128
CONV_C_W = 4
CONV_C_DIM = D_INNER + 2 * G_C * N_C
D_FF = 2816
D_PLE = 256
EPS = 1e-6
PAGE = 128

LANES = 128
SUBLANES = 8
VMEM_LIMIT = 56 * 1024 * 1024

OFF_A = 0
OFF_Q = 2 * C_A
OFF_K = OFF_Q + D_B
OFF_V = OFF_K + D_B
OFF_Z = OFF_V + D_B
OFF_X = OFF_Z + D_INNER
OFF_DT = OFF_X + CONV_C_DIM
W_IN_COLS = OFF_DT + LANES

FF_CHUNK = 256
SSD_CHUNK = 128
CONF_TILE = 512
CONF_ROWS = 64
CONF_PAD = 32
ATT_TILE = 256
HEADS_PER_STEP = LANES // DH_B
LOG2E = 1.4426950408889634


def _const_spec(shape):
    zeros = (0,) * len(shape)
    return pl.BlockSpec(shape, lambda *_: zeros, pipeline_mode=pl.Buffered(1))


def _layer_spec(shape, layer):
    zeros = (0,) * len(shape)
    return pl.BlockSpec((None,) + tuple(shape), lambda *_: (layer,) + zeros, pipeline_mode=pl.Buffered(1))


def _alias_prev(prev, first_in, first_out):
    if prev is None:
        return (), [], {}
    specs = [pl.BlockSpec(memory_space=pl.ANY) for _ in prev]
    return tuple(prev), specs, {first_in + j: first_out + j for j in range(len(prev))}


def _params(*sem):
    return pltpu.CompilerParams(dimension_semantics=sem, vmem_limit_bytes=VMEM_LIMIT)


def _sigmoid(x):
    return 0.5 * jnp.tanh(0.5 * x) + 0.5


def _silu(x):
    h = 0.5 * x
    return h + h * jnp.tanh(h)


def _softplus(x):
    return jnp.maximum(x, 0.0) + jnp.log(1.0 + jnp.exp(-jnp.abs(x)))


def _softplus2(x):
    sign = jnp.uint32(0x80000000)
    neg_abs = lax.bitcast_convert_type(lax.bitcast_convert_type(x, jnp.uint32) | sign, F32)
    return jnp.maximum(x, 0.0) + jnp.log(1.0 + jnp.exp2(neg_abs)) * LOG2E


def _rms(x, g):
    return x * lax.rsqrt(jnp.mean(x * x, axis=-1, keepdims=True) + EPS) * g


def _dot(a, b):
    return jnp.dot(a, b, preferred_element_type=F32)


def _dot_nt(a, b):
    return lax.dot_general(a, b, (((1,), (1,)), ((), ())), preferred_element_type=F32)


def _three_parts(x):
    hi = x.astype(BF16)
    rest = x - hi.astype(F32)
    mid = rest.astype(BF16)
    return hi, mid, (rest - mid.astype(F32)).astype(BF16)


def _spread(x, m):
    hi, mid, lo = _three_parts(x)
    return _dot(hi, m) + _dot(mid, m) + _dot(lo, m)


def _collect(m, x):
    hi, mid, lo = _three_parts(x)
    return _dot(m, hi) + _dot(m, mid) + _dot(m, lo)


def _split_dot(x, m):
    hi = x.astype(BF16)
    lo = (x - hi.astype(F32)).astype(BF16)
    return _dot(hi, m) + _dot(lo, m)


def _in_proj_kernel(h_ref, g_ref, w_ref, gq_ref, gk_ref, seg_ref, *rest):
    a_ref, q_ref, k_ref, v_ref, kb_ref, vb_ref, z_ref, x_ref, dt_ref = rest[-9:]
    u = _rms(h_ref[...], g_ref[...]).astype(BF16)

    def proj(lo, hi):
        return _dot(u, w_ref[:, lo:hi])

    a = proj(OFF_A, OFF_Q)
    a_ref[...] = a[:, :C_A] * _sigmoid(a[:, C_A:])
    z_ref[...] = proj(OFF_Z, OFF_X)
    dt_ref[...] = proj(OFF_DT, W_IN_COLS)
    x_ref[...] = proj(OFF_X, OFF_DT)

    seg = seg_ref[...]

    def head_norm(t, g):
        ms = _dot((t * t).astype(BF16), seg) * (1.0 / DH_B)
        return t * lax.rsqrt(ms + EPS) * g

    qn = head_norm(proj(OFF_Q, OFF_K), gq_ref[...])
    kn = head_norm(proj(OFF_K, OFF_V), gk_ref[...])
    v = proj(OFF_V, OFF_Z)
    q_ref[...] = (qn * (DH_B ** -0.5 * LOG2E)).astype(BF16)
    k_ref[0] = kn.T
    v_ref[0] = v.T
    kb_ref[...] = kn.astype(BF16)
    vb_ref[...] = v.astype(BF16)


def _in_proj(h, g_mix, w_in, gq, gk, seg, tm, batch, layer, prev_kv):
    t = h.shape[0]
    depth = w_in.shape[0]
    seq = t // batch
    nt = seq // tm
    row = lambda n: pl.BlockSpec((tm, n), lambda i: (i, 0))
    col = pl.BlockSpec((None, 1, D_B, tm), lambda i: (layer, i // nt, 0, i % nt))
    rows = lambda n, d: (row(n), jax.ShapeDtypeStruct((t, n), d))
    cols = (col, jax.ShapeDtypeStruct((depth, batch, D_B, seq), F32))
    outs = [rows(C_A, F32), rows(D_B, BF16), cols, cols, rows(D_B, BF16), rows(D_B, BF16),
            rows(D_INNER, F32), rows(CONV_C_DIM, F32), rows(LANES, F32)]
    prev, prev_specs, aliases = _alias_prev(prev_kv, 6, 2)
    return pl.pallas_call(
        _in_proj_kernel,
        grid=(t // tm,),
        in_specs=[row(D_MODEL), _const_spec((1, D_MODEL)), _layer_spec((D_MODEL, W_IN_COLS), layer),
                  _const_spec((1, D_B)), _const_spec((1, D_B)), _const_spec((D_B, D_B))] + prev_specs,
        out_specs=[spec for spec, _ in outs],
        out_shape=[shape for _, shape in outs],
        input_output_aliases=aliases,
        compiler_params=_params("parallel"),
        name="in_proj",
    )(h, g_mix, w_in, gq, gk, seg, *prev)


def _layer_norm_silu(x, g, b):
    mu = jnp.mean(x, axis=-1, keepdims=True)
    xc = x - mu
    y = xc * lax.rsqrt(jnp.mean(xc * xc, axis=-1, keepdims=True) + EPS)
    return _silu(y * g + b)


def _conf_prompt_kernel(a_ref, w_ref, b_ref, g_ref, bb_ref, o_ref, tail_ref, buf, shifted):
    tile = a_ref.shape[0]

    @pl.when(pl.program_id(1) == 0)
    def _():
        buf[0:CONF_PAD, :] = jnp.zeros((CONF_PAD, C_A), F32)

    buf[CONF_PAD:CONF_PAD + tile, :] = a_ref[...]
    n_shift = shifted.shape[1]
    for r in range(1, SUBLANES):
        shifted[r - 1] = buf[r:r + n_shift, :]
    first = CONF_PAD - (CONV_A_W - 1)
    for r0 in range(0, tile, CONF_ROWS):
        acc = jnp.zeros((CONF_ROWS, C_A), F32)
        for k in range(CONV_A_W):
            r = (first + k) % SUBLANES
            base = first + r0 + k - r
            rows = buf[base:base + CONF_ROWS, :] if r == 0 else shifted[r - 1, base:base + CONF_ROWS, :]
            acc = acc + w_ref[k:k + 1, :] * rows
        y = _layer_norm_silu(acc + b_ref[...], g_ref[...], bb_ref[...])
        o_ref[r0:r0 + CONF_ROWS, :] = y.astype(BF16)
    tail = buf[tile:tile + CONF_PAD, :]
    buf[0:CONF_PAD, :] = tail
    tail_ref[0] = tail


def _conf_prompt(glu, w, b, g, bb, batch):
    t = glu.shape[0]
    nt = t // batch // CONF_TILE
    return pl.pallas_call(
        _conf_prompt_kernel,
        grid=(batch, nt),
        in_specs=[pl.BlockSpec((CONF_TILE, C_A), lambda bi, ti: (bi * nt + ti, 0)),
                  _const_spec((CONF_PAD, C_A)), _const_spec((1, C_A)), _const_spec((1, C_A)),
                  _const_spec((1, C_A))],
        out_specs=[pl.BlockSpec((CONF_TILE, C_A), lambda bi, ti: (bi * nt + ti, 0)),
                   pl.BlockSpec((1, CONF_PAD, C_A), lambda bi, ti: (bi, 0, 0))],
        out_shape=[jax.ShapeDtypeStruct((t, C_A), BF16),
                   jax.ShapeDtypeStruct((batch, CONF_PAD, C_A), F32)],
        scratch_shapes=[pltpu.VMEM((CONF_PAD + CONF_TILE, C_A), F32),
                        pltpu.VMEM((SUBLANES - 1, CONF_PAD + CONF_TILE - SUBLANES, C_A), F32)],
        compiler_params=_params("parallel", "arbitrary"),
        name="conformer_prompt",
    )(glu, w, b, g, bb)


def _attn_prompt_kernel(bias_ref, q_ref, k_ref, v_ref, tri_ref, o_ref, s_ref, cs_ref, acc_ref, carry_ref):
    qi = pl.program_id(1)
    tile = q_ref.shape[0]
    tri = tri_ref[...]
    lane = lax.broadcasted_iota(jnp.int32, (tile, LANES), 1)
    row = lax.broadcasted_iota(jnp.int32, (tile, tile), 0)
    col = lax.broadcasted_iota(jnp.int32, (tile, tile), 1)
    causal = col < row

    heads = []
    for lt in range(D_B // LANES):
        q2 = q_ref[:, lt * LANES:(lt + 1) * LANES].astype(F32)
        for hh in range(HEADS_PER_STEP):
            qh = jnp.where((lane // DH_B) == hh, q2, 0.0).astype(BF16)
            heads.append((lt, qh, bias_ref[lt * HEADS_PER_STEP + hh]))
    n = len(heads)

    def scores(i, start):
        lt, qh, bias = heads[i]
        s_ref[i] = _dot_nt(qh, k_ref[pl.ds(start, tile), lt * LANES:(lt + 1) * LANES]) + bias

    def later_sums(i, masked):
        sp = _softplus2(s_ref[i])
        if masked:
            sp = jnp.where(causal, sp, 0.0)
        cs_ref[i] = _dot(sp.astype(BF16), tri)

    def weigh(i, start, masked):
        lt = heads[i][0]
        cs = cs_ref[i]
        carry = carry_ref[i]
        w = jnp.exp2(s_ref[i] - cs - carry)
        if masked:
            w = jnp.where(causal, w, 0.0)
        acc_ref[i] += _dot(w.astype(BF16), v_ref[pl.ds(start, tile), lt * LANES:(lt + 1) * LANES])
        carry_ref[i] = carry + cs[:, 0:1]

    acc_ref[...] = jnp.zeros_like(acc_ref)
    carry_ref[...] = jnp.zeros_like(carry_ref)
    start0 = pl.multiple_of(qi * tile, tile)
    for i in range(n):
        scores(i, start0)
    for i in range(n):
        later_sums(i, True)
    nxt0 = pl.multiple_of(jnp.maximum(qi - 1, 0) * tile, tile)
    for i in range(n):
        weigh(i, start0, True)
        scores(i, nxt0)
    for i in range(n):
        later_sums(i, False)

    @pl.loop(0, qi - 1)
    def _(j):
        cur = qi - 1 - j
        for i in range(n):
            weigh(i, pl.multiple_of(cur * tile, tile), False)
            scores(i, pl.multiple_of((cur - 1) * tile, tile))
        for i in range(n):
            later_sums(i, False)

    @pl.when(qi > 0)
    def _():
        for i in range(n):
            weigh(i, 0, False)

    for lt in range(D_B // LANES):
        out = acc_ref[lt * HEADS_PER_STEP]
        for hh in range(1, HEADS_PER_STEP):
            out = jnp.where((lane // DH_B) == hh, acc_ref[lt * HEADS_PER_STEP + hh], out)
        o_ref[:, lt * LANES:(lt + 1) * LANES] = out.astype(BF16)


def _attn_prompt(bias, q, k, v, tri, batch):
    t = q.shape[0]
    seq = t // batch
    nq = seq // ATT_TILE
    resident = lambda: pl.BlockSpec((seq, D_B), lambda b, i, *_: (b, 0), pipeline_mode=pl.Buffered(1))
    grid_spec = pltpu.PrefetchScalarGridSpec(
        num_scalar_prefetch=1,
        grid=(batch, nq),
        in_specs=[pl.BlockSpec((ATT_TILE, D_B), lambda b, i, *_: (b * nq + i, 0)),
                  resident(), resident(),
                  pl.BlockSpec((ATT_TILE, ATT_TILE), lambda b, i, *_: (0, 0))],
        out_specs=pl.BlockSpec((ATT_TILE, D_B), lambda b, i, *_: (b * nq + i, 0)),
        scratch_shapes=[pltpu.VMEM((H_B, ATT_TILE, ATT_TILE), F32),
                        pltpu.VMEM((H_B, ATT_TILE, ATT_TILE), F32),
                        pltpu.VMEM((H_B, ATT_TILE, LANES), F32),
                        pltpu.VMEM((H_B, ATT_TILE, 1), F32)],
    )
    return pl.pallas_call(
        _attn_prompt_kernel,
        grid_spec=grid_spec,
        out_shape=jax.ShapeDtypeStruct((t, D_B), BF16),
        compiler_params=_params("parallel", "arbitrary"),
        name="attn_prompt",
    )(bias, q, k, v, tri)


def _gated_group_norm(y, xs, z, dskip, g):
    y = (y + dskip * xs) * _silu(z)
    gw = D_INNER // G_C
    return jnp.concatenate(
        [_rms(y[:, i * gw:(i + 1) * gw], g[:, i * gw:(i + 1) * gw]) for i in range(G_C)], axis=1)


def _ssd_prompt_kernel(x_ref, z_ref, dt_ref, cw_ref, cb_ref, dtb_ref, alog_ref, dskip_ref, g_ref,
                       tril_ref, e_ref, c_ref, st_ref, cbuf, ybuf):
    q = x_ref.shape[0]

    @pl.when(pl.program_id(1) == 0)
    def _():
        cbuf[0:SUBLANES, :] = jnp.zeros((SUBLANES, CONV_C_DIM), F32)
        st_ref[...] = jnp.zeros_like(st_ref)

    cbuf[SUBLANES:SUBLANES + q, :] = x_ref[...]
    first = SUBLANES - (CONV_C_W - 1)
    conv = cb_ref[...] + sum(cw_ref[k:k + 1, :] * cbuf[first + k:first + k + q, :]
                             for k in range(CONV_C_W))
    cbuf[0:SUBLANES, :] = cbuf[q:q + SUBLANES, :]
    xbc = _silu(conv)
    xs = xbc[:, :D_INNER]
    bm = xbc[:, D_INNER:D_INNER + G_C * N_C].astype(BF16)
    cm = xbc[:, D_INNER + G_C * N_C:].astype(BF16)

    dt = _softplus(dt_ref[...] + dtb_ref[...])
    a_cum = _collect(tril_ref[...], dt * (-jnp.exp(alog_ref[...])))
    e = e_ref[...]
    dt_x = _spread(dt, e)
    a_x = _spread(a_cum, e)
    a_cum_t = a_cum.T
    a_xt = a_x.T
    xdt = xs * dt_x
    exp_a_x = jnp.exp(a_x)

    row = lax.broadcasted_iota(jnp.int32, (q, q), 0)
    col = lax.broadcasted_iota(jnp.int32, (q, q), 1)
    causal = col <= row
    lane = lax.broadcasted_iota(jnp.int32, (q, LANES), 1)
    heads_per_group = H_C // G_C

    for g in range(G_C):
        bg = bm[:, g * N_C:(g + 1) * N_C]
        cg = cm[:, g * N_C:(g + 1) * N_C]
        cb = _dot_nt(cg, bg)
        for pp in range(heads_per_group // HEADS_PER_STEP):
            pair = g * (heads_per_group // HEADS_PER_STEP) + pp
            lo = pair * LANES
            xdt_p = xdt[:, lo:lo + LANES]
            xdt_pb = xdt_p.astype(BF16)
            y_p = None
            for hh in range(HEADS_PER_STEP):
                h = pair * HEADS_PER_STEP + hh
                seg = a_cum[:, h:h + 1] - a_cum_t[h:h + 1, :]
                m = (cb * jnp.exp(jnp.where(causal, seg, -jnp.inf))).astype(BF16)
                r = _dot(m, xdt_pb)
                y_p = r if y_p is None else jnp.where((lane // P_C) == hh, r, y_p)
            h_prev = st_ref[0, lo:lo + LANES, :]
            y_p = y_p + _dot_nt(cg, h_prev.astype(BF16)) * exp_a_x[:, lo:lo + LANES]
            ybuf[:, lo:lo + LANES] = y_p
            a_t = a_xt[lo:lo + LANES, :]
            tot = a_t[:, q - 1:q]
            s_new = _dot((xdt_p.T * jnp.exp(tot - a_t)).astype(BF16), bg)
            st_ref[0, lo:lo + LANES, :] = jnp.exp(tot) * h_prev + s_new

    c_ref[...] = _gated_group_norm(ybuf[...], xs, z_ref[...], dskip_ref[...], g_ref[...]).astype(BF16)


def _ssd_prompt(xbc, z, dt, cw, cb, dtb, alog, dskip, g, tril, e, batch):
    t = xbc.shape[0]
    nc = t // batch // SSD_CHUNK
    row = lambda n: pl.BlockSpec((SSD_CHUNK, n), lambda bi, ci: (bi * nc + ci, 0))
    return pl.pallas_call(
        _ssd_prompt_kernel,
        grid=(batch, nc),
        in_specs=[row(CONV_C_DIM), row(D_INNER), row(LANES),
                  _const_spec((SUBLANES, CONV_C_DIM)), _const_spec((1, CONV_C_DIM)),
                  _const_spec((1, LANES)), _const_spec((1, LANES)), _const_spec((1, D_INNER)),
                  _const_spec((1, D_INNER)), _const_spec((SSD_CHUNK, SSD_CHUNK)),
                  _const_spec((LANES, D_INNER))],
        out_specs=[row(D_INNER), pl.BlockSpec((1, D_INNER, N_C), lambda bi, ci: (bi, 0, 0))],
        out_shape=[jax.ShapeDtypeStruct((t, D_INNER), BF16),
                   jax.ShapeDtypeStruct((batch, D_INNER, N_C), F32)],
        scratch_shapes=[pltpu.VMEM((SUBLANES + SSD_CHUNK, CONV_C_DIM), F32),
                        pltpu.VMEM((SSD_CHUNK, D_INNER), F32)],
        compiler_params=_params("parallel", "arbitrary"),
        name="ssd_prompt",
    )(xbc, z, dt, cw, cb, dtb, alog, dskip, g, tril, e)


def _post_kernel(h_ref, a_ref, b_ref, c_ref, pe_ref, wo_ref, gf_ref, wg_ref, wu_ref, wd_ref,
                 gp_ref, wpg_ref, wp_ref, o_ref):
    h = (h_ref[...] + _dot(a_ref[...], wo_ref[0:C_A, :]) + _dot(b_ref[...], wo_ref[C_A:C_A + D_B, :])
         + _dot(c_ref[...], wo_ref[C_A + D_B:, :]))
    f = _rms(h, gf_ref[...]).astype(BF16)
    acc = jnp.zeros_like(h)
    for c0 in range(0, D_FF, FF_CHUNK):
        gate = _dot(f, wg_ref[:, c0:c0 + FF_CHUNK])
        up = _dot(f, wu_ref[:, c0:c0 + FF_CHUNK])
        acc = acc + _dot((_silu(gate) * up).astype(BF16), wd_ref[c0:c0 + FF_CHUNK, :])
    h = h + acc
    gate = _sigmoid(_dot(_rms(h, gp_ref[...]).astype(BF16), wpg_ref[...]))
    o_ref[...] = h + _dot(pe_ref[...].astype(BF16), wp_ref[...]) * gate


def _post(h, a, b, c, pe, wo, gf, wg, wu, wd, gp, wpg, wp, tm, layer):
    t = h.shape[0]
    row = lambda n: pl.BlockSpec((tm, n), lambda i: (i, 0))
    weight = lambda w: _layer_spec(w.shape[1:], layer)
    return pl.pallas_call(
        _post_kernel,
        grid=(t // tm,),
        in_specs=[row(D_MODEL), row(C_A), row(D_B), row(D_INNER),
                  pl.BlockSpec((None, tm, D_PLE), lambda i: (layer, i, 0)),
                  weight(wo), _const_spec((1, D_MODEL)), weight(wg), weight(wu), weight(wd),
                  _const_spec((1, D_MODEL)), weight(wpg), weight(wp)],
        out_specs=row(D_MODEL),
        out_shape=jax.ShapeDtypeStruct((t, D_MODEL), F32),
        compiler_params=_params("parallel"),
        name="post",
    )(h, a, b, c, pe, wo, gf, wg, wu, wd, gp, wpg, wp)


def _conf_sample_kernel(a_ref, st_ref, w_ref, b_ref, g_ref, bb_ref, o_ref, nst_ref):
    glu = a_ref[...]
    hist = CONV_A_W - 1
    acc = b_ref[...] + w_ref[hist:hist + 1, :] * glu
    for k in range(hist):
        acc = acc + w_ref[k:k + 1, :] * st_ref[k]
    o_ref[...] = _layer_norm_silu(acc, g_ref[...], bb_ref[...]).astype(BF16)
    for k in range(hist - 1):
        nst_ref[k] = st_ref[k + 1]
    nst_ref[hist - 1] = glu


def _conf_sample(glu, state, w, b, g, bb, nb):
    n = glu.shape[0]
    hist = CONV_A_W - 1
    row = lambda m: pl.BlockSpec((nb, m), lambda i: (i, 0))
    taps = pl.BlockSpec((hist, nb, C_A), lambda i: (0, i, 0))
    return pl.pallas_call(
        _conf_sample_kernel,
        grid=(n // nb,),
        in_specs=[row(C_A), taps, _const_spec((CONF_PAD, C_A)), _const_spec((1, C_A)),
                  _const_spec((1, C_A)), _const_spec((1, C_A))],
        out_specs=[row(C_A), taps],
        out_shape=[jax.ShapeDtypeStruct((n, C_A), BF16), jax.ShapeDtypeStruct((hist, n, C_A), F32)],
        compiler_params=_params("parallel"),
        name="conformer_sample",
    )(glu, state, w, b, g, bb)


def _attn_sample_kernel(n_pages, per_step, pt_ref, q_ref, bias_ref, tri_ref, later_ref, *refs):
    o_ref = refs[2 * per_step * n_pages]
    own = (lax.broadcasted_iota(jnp.int32, (H_B, D_B), 1) // DH_B
           == lax.broadcasted_iota(jnp.int32, (H_B, D_B), 0))
    for r in range(per_step):
        k_refs = refs[r * n_pages:(r + 1) * n_pages]
        v_refs = refs[(per_step + r) * n_pages:(per_step + r + 1) * n_pages]
        q_rows = jnp.broadcast_to(q_ref[r].astype(F32), (H_B, D_B))
        qmat = jnp.where(own, q_rows, 0.0).astype(BF16)
        s = jnp.concatenate([_dot(qmat, k_refs[pg][0, 0].astype(BF16)) for pg in range(n_pages)], axis=0)
        s = s + bias_ref[...]
        cs = _split_dot(_softplus2(s), tri_ref[...])
        page_tot = jnp.broadcast_to(cs[:, 0:1], cs.shape)
        cs = cs + _collect(later_ref[...], page_tot)
        w = jnp.exp2(s - cs).astype(BF16)
        acc = jnp.zeros((H_B, D_B), F32)
        for pg in range(n_pages):
            acc = acc + _dot_nt(w[pg * H_B:(pg + 1) * H_B, :], v_refs[pg][0, 0].astype(BF16))
        o_ref[r] = jnp.sum(jnp.where(own, acc, 0.0), axis=0, keepdims=True).astype(BF16)


def _attn_sample(page_table, q, bias, cache_kt, cache_vt, layer, per_step):
    n_seq, n_pages = page_table.shape
    rows = n_pages * H_B
    r = np.arange(rows)
    later = _const01((r[:, None] % H_B == r[None, :] % H_B) & (r[None, :] // H_B > r[:, None] // H_B))
    bias_rows = jnp.broadcast_to(jnp.tile(bias, n_pages)[:, None], (rows, PAGE))
    page_specs = lambda: [pl.BlockSpec((1, 1, D_B, PAGE),
                                       lambda s, pt, j=j, pg=pg: (layer, pt[(s * per_step + j) * n_pages + pg], 0, 0))
                          for j in range(per_step) for pg in range(n_pages)]
    const = lambda shape: pl.BlockSpec(shape, lambda s, pt: (0,) * len(shape))
    grid_spec = pltpu.PrefetchScalarGridSpec(
        num_scalar_prefetch=1,
        grid=(n_seq // per_step,),
        in_specs=[pl.BlockSpec((per_step, 1, D_B), lambda s, pt: (s, 0, 0)),
                  const((rows, PAGE)), const((PAGE, PAGE)), const((rows, rows))]
                 + page_specs() + page_specs(),
        out_specs=pl.BlockSpec((per_step, 1, D_B), lambda s, pt: (s, 0, 0)),
    )
    n_ops = per_step * n_pages
    out = pl.pallas_call(
        functools.partial(_attn_sample_kernel, n_pages, per_step),
        grid_spec=grid_spec,
        out_shape=jax.ShapeDtypeStruct((n_seq, 1, D_B), BF16),
        compiler_params=_params("arbitrary"),
        name="attn_sample",
    )(page_table.reshape(-1), q.reshape(n_seq, 1, D_B), bias_rows, _tri_keys_after(PAGE), later,
      *([cache_kt] * n_ops), *([cache_vt] * n_ops))
    return out.reshape(n_seq, D_B)


def _ssd_sample_prep_kernel(x_ref, st_ref, dt_ref, cw_ref, cb_ref, dtb_ref, alog_ref, e_ref,
                            nst_ref, xs_ref, xdt_t_ref, dec_t_ref, b_ref, c_ref):
    x = x_ref[...]
    hist = CONV_C_W - 1
    conv = cb_ref[...] + cw_ref[hist:hist + 1, :] * x
    for k in range(hist):
        conv = conv + cw_ref[k:k + 1, :] * st_ref[k]
    for k in range(hist - 1):
        nst_ref[k] = st_ref[k + 1]
    nst_ref[hist - 1] = x
    xbc = _silu(conv)
    xs = xbc[:, :D_INNER]
    xs_ref[...] = xs
    b_ref[...] = xbc[:, D_INNER:D_INNER + G_C * N_C]
    c_ref[...] = xbc[:, D_INNER + G_C * N_C:]
    dt = _softplus(dt_ref[...] + dtb_ref[...])
    e = e_ref[...]
    xdt_t_ref[...] = (xs * _spread(dt, e)).T
    dec_t_ref[...] = jnp.exp(_spread(dt * (-jnp.exp(alog_ref[...])), e)).T


def _ssd_sample_prep(xbc, state, dt, cw, cb, dtb, alog, e):
    n = xbc.shape[0]
    shapes = [(state.shape, F32), ((n, D_INNER), F32), ((D_INNER, n), F32), ((D_INNER, n), F32),
              ((n, G_C * N_C), F32), ((n, G_C * N_C), F32)]
    full = lambda s: pl.BlockSpec(s, lambda i: (0,) * len(s))
    return pl.pallas_call(
        _ssd_sample_prep_kernel,
        grid=(1,),
        in_specs=[full(xbc.shape), full(state.shape), full(dt.shape), full(cw.shape), full(cb.shape),
                  full(dtb.shape), full(alog.shape), full(e.shape)],
        out_specs=[full(s) for s, _ in shapes],
        out_shape=[jax.ShapeDtypeStruct(s, d) for s, d in shapes],
        compiler_params=_params("arbitrary"),
        name="ssd_sample_prep",
    )(xbc, state, dt, cw, cb, dtb, alog, e)


def _ssd_sample_state_kernel(xdt_t_ref, dec_t_ref, b_ref, c_ref, h_ref, *rest):
    nh_ref, y_t_ref = rest[-2:]
    n = xdt_t_ref.shape[1]
    per_step = h_ref.shape[0]
    rows = D_INNER // G_C

    @pl.when(pl.program_id(0) == 0)
    def _():
        y_t_ref[...] = jnp.zeros_like(y_t_ref)

    for jj in range(per_step):
        j = pl.program_id(0) * per_step + jj
        mine = lax.broadcasted_iota(jnp.int32, (rows, n), 1) == j
        for g in range(G_C):
            r0 = g * rows
            pick = lambda ref: jnp.sum(jnp.where(mine, ref[r0:r0 + rows, :], 0.0), axis=1, keepdims=True)
            h_new = (pick(dec_t_ref) * h_ref[jj, r0:r0 + rows, :]
                     + pick(xdt_t_ref) * b_ref[jj, :, g * N_C:(g + 1) * N_C])
            nh_ref[jj, r0:r0 + rows, :] = h_new
            y = jnp.sum(h_new * c_ref[jj, :, g * N_C:(g + 1) * N_C], axis=1, keepdims=True)
            y_t_ref[r0:r0 + rows, :] = jnp.where(mine, y, y_t_ref[r0:r0 + rows, :])


def _ssd_sample_state(xdt_t, dec_t, bm, cm, state, layer, prev_state, per_step):
    depth, n = state.shape[:2]
    full = lambda s: pl.BlockSpec(s, lambda j: (0,) * len(s))
    per_seq = pl.BlockSpec((per_step, 1, G_C * N_C), lambda j: (j, 0, 0))
    slab = pl.BlockSpec((None, per_step, D_INNER, N_C), lambda j: (layer, j, 0, 0))
    prev, prev_specs, aliases = _alias_prev(prev_state, 5, 0)
    return pl.pallas_call(
        _ssd_sample_state_kernel,
        grid=(n // per_step,),
        in_specs=[full(xdt_t.shape), full(dec_t.shape), per_seq, per_seq, slab] + prev_specs,
        out_specs=[slab, full((D_INNER, n))],
        out_shape=[jax.ShapeDtypeStruct(state.shape, F32), jax.ShapeDtypeStruct((D_INNER, n), F32)],
        input_output_aliases=aliases,
        compiler_params=_params("arbitrary"),
        name="ssd_sample_state",
    )(xdt_t, dec_t, bm.reshape(n, 1, -1), cm.reshape(n, 1, -1), state, *prev)


def _ssd_sample_out_kernel(y_t_ref, xs_ref, z_ref, dskip_ref, g_ref, c_ref):
    c_ref[...] = _gated_group_norm(y_t_ref[...].T, xs_ref[...], z_ref[...], dskip_ref[...],
                                   g_ref[...]).astype(BF16)


def _ssd_sample_out(y_t, xs, z, dskip, g):
    n = xs.shape[0]
    full = lambda s: pl.BlockSpec(s, lambda i: (0,) * len(s))
    return pl.pallas_call(
        _ssd_sample_out_kernel,
        grid=(1,),
        in_specs=[full(y_t.shape), full(xs.shape), full(z.shape), full(dskip.shape), full(g.shape)],
        out_specs=full((n, D_INNER)),
        out_shape=jax.ShapeDtypeStruct((n, D_INNER), BF16),
        compiler_params=_params("arbitrary"),
        name="ssd_sample_out",
    )(y_t, xs, z, dskip, g)


def _token_tile(t):
    for tile in (512, 256):
        if t % tile == 0:
            return tile
    return t


def _const01(mask):
    return jnp.asarray(np.asarray(mask, np.float32), BF16)


def _tri_keys_after(n):
    i = np.arange(n)
    return _const01(i[:, None] >= i[None, :])


def kernel(x_prompt, x_sample, cache_k, cache_v, state_conv_a, state_conv_ssm, state_ssm, page_table,
           p_prompt, p_sample, g_mix, w_in, conv_a_w, conv_a_b, ln_a_g, ln_a_b, g_q, g_k, sb_bias,
           conv_ssm_w, conv_ssm_b, dt_bias, a_log, d_skip, g_ssm, w_out, g_ffn, w_ffn_gate, w_ffn_up,
           w_ffn_down, g_ple, w_ple_gate, w_ple):
    depth = w_in.shape[0]
    batch, seq, _ = x_prompt.shape
    n_seq = x_sample.shape[0]
    tp = batch * seq
    assert x_sample.shape[1] == 1
    assert seq % CONF_TILE == 0 and seq % ATT_TILE == 0 and seq % SSD_CHUNK == 0

    pad_lanes = lambda v: jnp.pad(v, ((0, 0), (0, LANES - v.shape[1])))
    seg = _const01(np.arange(D_B)[:, None] // DH_B == np.arange(D_B)[None, :] // DH_B)
    head_lanes = _const01(np.arange(LANES)[:, None] == np.arange(D_INNER)[None, :] // P_C)
    tril = _const01(np.arange(SSD_CHUNK)[:, None] >= np.arange(SSD_CHUNK)[None, :])
    tri_att = _tri_keys_after(ATT_TILE)

    n_pool = cache_k.shape[1]
    cache_kt = jnp.transpose(cache_k, (0, 1, 3, 4, 2)).reshape(depth, n_pool, D_B, PAGE)
    cache_vt = jnp.transpose(cache_v, (0, 1, 3, 4, 2)).reshape(depth, n_pool, D_B, PAGE)
    conv_a_taps = jnp.transpose(state_conv_a, (0, 2, 1, 3))
    conv_c_taps = jnp.transpose(state_conv_ssm, (0, 2, 1, 3))
    ssm_flat = state_ssm.reshape(depth, n_seq, D_INNER, N_C)

    h_p = x_prompt.reshape(tp, D_MODEL)
    h_s = x_sample.reshape(n_seq, D_MODEL)
    pe_p = p_prompt.reshape(depth, tp, D_PLE)
    pe_s = p_sample.reshape(depth, n_seq, D_PLE)
    w_in_b = jnp.pad(w_in, ((0, 0), (0, 0), (0, W_IN_COLS - w_in.shape[2]))).astype(BF16)
    wo, wg, wu, wd, wpg, wp = (w.astype(BF16) for w in (w_out, w_ffn_gate, w_ffn_up, w_ffn_down,
                                                        w_ple_gate, w_ple))
    tile_p, tile_s = _token_tile(seq), _token_tile(n_seq)
    seqs_per_step = 8 if n_seq % 8 == 0 else 1
    kv_p = kv_s = ssm_s = None
    outs = {name: [] for name in ("cap", "ccp", "ssp", "cas", "ccs")}
    for i in range(depth):
        g_mix_i = g_mix[i][None]
        gq = jnp.tile(g_q[i], H_B)[None]
        gk = jnp.tile(g_k[i], H_B)[None]
        conv_a_w_i = jnp.pad(conv_a_w[i], ((0, CONF_PAD - CONV_A_W), (0, 0)))
        conv_c_w_i = jnp.pad(conv_ssm_w[i], ((0, SUBLANES - CONV_C_W), (0, 0)))
        dtb = pad_lanes(dt_bias[i][None])
        alog = pad_lanes(a_log[i][None])
        dskip = jnp.repeat(d_skip[i], P_C)[None]
        bias2 = sb_bias[i] * LOG2E
        post_w = (wo, g_ffn[i][None], wg, wu, wd, g_ple[i][None], wpg, wp)

        glu, q, kt, vt, kb, vb, z, xbc, dt = _in_proj(h_p, g_mix_i, w_in_b, gq, gk, seg, tile_p,
                                                      batch, i, kv_p)
        kv_p = (kt, vt)
        a_out, a_tail = _conf_prompt(glu, conv_a_w_i, conv_a_b[i][None], ln_a_g[i][None],
                                     ln_a_b[i][None], batch)
        b_out = _attn_prompt(bias2, q, kb, vb, tri_att, batch)
        c_out, ssm_p = _ssd_prompt(xbc, z, dt, conv_c_w_i, conv_ssm_b[i][None], dtb, alog, dskip,
                                   g_ssm[i][None], tril, head_lanes, batch)
        h_p = _post(h_p, a_out, b_out, c_out, pe_p, *post_w, tile_p, i)
        outs["cap"].append(a_tail[:, CONF_PAD - (CONV_A_W - 1):])
        outs["ccp"].append(xbc.reshape(batch, seq, CONV_C_DIM)[:, seq - (CONV_C_W - 1):])
        outs["ssp"].append(ssm_p.reshape(batch, H_C, P_C, N_C))

        glu, q, kt, vt, _, _, z, xbc, dt = _in_proj(h_s, g_mix_i, w_in_b, gq, gk, seg, tile_s,
                                                    1, i, kv_s)
        kv_s = (kt, vt)
        a_out, conv_a_new = _conf_sample(glu, conv_a_taps[i], conv_a_w_i, conv_a_b[i][None],
                                         ln_a_g[i][None], ln_a_b[i][None], min(n_seq, 32))
        b_out = _attn_sample(page_table, q, bias2, cache_kt, cache_vt, i, 2 if n_seq % 2 == 0 else 1)
        conv_c_new, xs, xdt_t, dec_t, bm, cm = _ssd_sample_prep(
            xbc, conv_c_taps[i], dt, conv_c_w_i, conv_ssm_b[i][None], dtb, alog, head_lanes)
        ssm_s, y_t = _ssd_sample_state(xdt_t, dec_t, bm, cm, ssm_flat, i,
                                       None if ssm_s is None else (ssm_s,), seqs_per_step)
        c_out = _ssd_sample_out(y_t, xs, z, dskip, g_ssm[i][None])
        h_s = _post(h_s, a_out, b_out, c_out, pe_s, *post_w, tile_s, i)
        outs["cas"].append(jnp.transpose(conv_a_new, (1, 0, 2)))
        outs["ccs"].append(jnp.transpose(conv_c_new, (1, 0, 2)))

    st = lambda name: jnp.stack(outs[name])
    heads_last = lambda x: jnp.transpose(x.reshape(depth, x.shape[1], H_B, DH_B, x.shape[3]), (0, 1, 4, 2, 3))
    k_p, v_p = (heads_last(x) for x in kv_p)
    k_s, v_s = (jnp.transpose(heads_last(x), (0, 2, 1, 3, 4)) for x in kv_s)
    return (h_p.reshape(batch, seq, D_MODEL), h_s.reshape(n_seq, 1, D_MODEL),
            k_p, v_p, st("cap"), st("ccp"), st("ssp"),
            k_s, v_s, st("cas"), st("ccs"), ssm_s.reshape(depth, n_seq, H_C, P_C, N_C))
```
